```python
import jax, jax.numpy as jnp
from jax import lax
import numpy as np

D_MODEL = 2048
BATCH = 4
SEQ = 8192
DEPTH = 4

N_MIXERS = 3

D_FF = -(-8 * D_MODEL // (3 * 256)) * 256

POOL_WINDOWS = (2, 4, 8, 16)
POOL_GROUPS = len(POOL_WINDOWS)
POOL_DIM = D_MODEL
POOL_GROUP_DIM = POOL_DIM // POOL_GROUPS

SGU_CHUNK = 128
SGU_DIM = D_MODEL
SGU_HEADS = 8
SGU_HEAD_DIM = SGU_DIM // SGU_HEADS

GLA_HEADS = 4
GLA_KEY_DIM = D_MODEL // 2
GLA_VALUE_DIM = D_MODEL
GLA_HEAD_K = GLA_KEY_DIM // GLA_HEADS
GLA_HEAD_V = GLA_VALUE_DIM // GLA_HEADS
GLA_GATE_RANK = 16
GLA_GATE_TEMP = 16.0
GLA_CHUNK = 64
GLA_IN_DIM = 2 * GLA_KEY_DIM + 2 * GLA_VALUE_DIM + GLA_GATE_RANK

EPS = 1e-6

kernel_name = "hybrid_pool_sgu_gla_trunk"


def rms_norm(x, w):
    xf = x.astype(jnp.float32)
    y = xf * lax.rsqrt(jnp.mean(xf * xf, axis=-1, keepdims=True) + EPS)
    return y.astype(x.dtype) * w


def layer_norm(x, w, b):
    xf = x.astype(jnp.float32)
    mu = jnp.mean(xf, axis=-1, keepdims=True)
    xc = xf - mu
    y = xc * lax.rsqrt(jnp.mean(xc * xc, axis=-1, keepdims=True) + EPS)
    return y.astype(x.dtype) * w + b


def swiglu_ffn(h, w_gate, w_up, w_down):
    return (jax.nn.silu(h @ w_gate) * (h @ w_up)) @ w_down


def pool_mixer(h, w_in, w_group, scale, w_out):
    z = h @ w_in
    B, S, _ = z.shape
    zf = z.astype(jnp.float32)
    cs = jnp.cumsum(zf, axis=1)
    t = jnp.arange(S)
    outs = []
    for g, w in enumerate(POOL_WINDOWS):
        lo, hi = g * POOL_GROUP_DIM, (g + 1) * POOL_GROUP_DIM
        c = cs[..., lo:hi]
        c_prev = jnp.pad(c, ((0, 0), (w, 0), (0, 0)))[:, :S]
        count = jnp.minimum(t + 1, w).astype(jnp.float32)[None, :, None]
        outs.append((c - c_prev) / count - zf[..., lo:hi])
    p = jnp.stack(outs, axis=2).astype(z.dtype)
    y = jnp.einsum('bsgc,gcd->bsgd', p, w_group).reshape(B, S, POOL_DIM)
    return (y * scale) @ w_out


def sgu_mixer(h, w_in, v_norm_w, v_norm_b, w_spatial, b_spatial, w_out):
    z = jax.nn.gelu(h @ w_in, approximate=False)
    u, v = jnp.split(z, 2, axis=-1)
    v = layer_norm(v, v_norm_w, v_norm_b)
    B, S, _ = v.shape
    nc = S // SGU_CHUNK
    v = v.reshape(B, nc, SGU_CHUNK, SGU_HEADS, SGU_HEAD_DIM)
    mask = jnp.tril(jnp.ones((SGU_CHUNK, SGU_CHUNK), dtype=bool))
    w = jnp.where(mask[None], w_spatial, 0)
    mixed = jnp.einsum('hts,bnshd->bnthd', w, v) + b_spatial.T[None, None, :, :, None]
    gate = mixed.reshape(B, S, SGU_DIM)
    return (u * gate) @ w_out


def gla_mixer(h, w_in, w_gk_up, b_gk, out_norm_w, w_out):
    B, S, _ = h.shape
    z = h @ w_in
    q, k, v, r, gk_low = jnp.split(
        z, [GLA_KEY_DIM, 2 * GLA_KEY_DIM, 2 * GLA_KEY_DIM + GLA_VALUE_DIM,
            2 * GLA_KEY_DIM + 2 * GLA_VALUE_DIM], axis=-1)
    log_a = jax.nn.log_sigmoid((gk_low @ w_gk_up + b_gk).astype(jnp.float32)) / GLA_GATE_TEMP
    nc = S // GLA_CHUNK

    def heads(t, d):
        return t.astype(jnp.float32).reshape(B, nc, GLA_CHUNK, GLA_HEADS, d).transpose(0, 3, 1, 2, 4)

    q = heads(q, GLA_HEAD_K) * (GLA_HEAD_K ** -0.5)
    k = heads(k, GLA_HEAD_K)
    v = heads(v, GLA_HEAD_V)
    g = heads(log_a, GLA_HEAD_K)
    b = jnp.cumsum(g, axis=3)
    b_last = b[:, :, :, -1:, :]
    q_in = q * jnp.exp(b)
    k_in = k * jnp.exp(-b)
    k_dec = k * jnp.exp(b_last - b)

    mask = jnp.tril(jnp.ones((GLA_CHUNK, GLA_CHUNK), dtype=bool))
    scores = jnp.einsum('bhntd,bhnsd->bhnts', q_in, k_in)
    scores = jnp.where(mask, scores, 0.0)
    o_intra = jnp.einsum('bhnts,bhnsv->bhntv', scores, v)

    xs = (jnp.moveaxis(q_in, 2, 0), jnp.moveaxis(k_dec, 2, 0), jnp.moveaxis(v, 2, 0),
          jnp.moveaxis(jnp.exp(b_last[:, :, :, 0, :]), 2, 0))

    def step(state, inp):
        qc, kc, vc, dc = inp
        o = jnp.einsum('bhtd,bhdv->bhtv', qc, state)
        new_state = dc[..., None] * state + jnp.einsum('bhsd,bhsv->bhdv', kc, vc)
        return new_state, o

    s0 = jnp.zeros((B, GLA_HEADS, GLA_HEAD_K, GLA_HEAD_V), jnp.float32)
    _, o_inter = lax.scan(step, s0, xs)
    o = o_intra + jnp.moveaxis(o_inter, 0, 2)
    o = o.transpose(0, 2, 3, 1, 4).reshape(B, S, GLA_HEADS, GLA_HEAD_V)
    o = o * lax.rsqrt(jnp.mean(o * o, axis=-1, keepdims=True) + EPS)
    o = o.reshape(B, S, GLA_VALUE_DIM).astype(h.dtype) * out_norm_w
    o = o * jax.nn.silu(r)
    return o @ w_out


MIXERS = (pool_mixer, sgu_mixer, gla_mixer)


def setup_inputs(seed: int = 0) -> dict:
    key = jax.random.key(seed)
    keys = list(jax.random.split(key, 64))

    def nrm(shape, scale):
        return jax.random.normal(keys.pop(), shape, jnp.float32) * scale

    def gain(n):
        return 1.0 + nrm((n,), 0.02)

    p = {"x": nrm((BATCH, SEQ, D_MODEL), 1.0)}

    def ffn(prefix):
        p[prefix + "_norm2"] = gain(D_MODEL)
        p[prefix + "_ffn_w_gate"] = nrm((D_MODEL, D_FF), D_MODEL ** -0.5)
        p[prefix + "_ffn_w_up"] = nrm((D_MODEL, D_FF), D_MODEL ** -0.5)
        p[prefix + "_ffn_w_down"] = nrm((D_FF, D_MODEL), D_FF ** -0.5)

    def pool(prefix):
        p[prefix + "_norm1"] = gain(D_MODEL)
        p[prefix + "_pool_w_in"] = nrm((D_MODEL, POOL_DIM), D_MODEL ** -0.5)
        p[prefix + "_pool_w_group"] = nrm((POOL_GROUPS, POOL_GROUP_DIM, POOL_GROUP_DIM), POOL_GROUP_DIM ** -0.5)
        p[prefix + "_pool_scale"] = gain(POOL_DIM)
        p[prefix + "_pool_w_out"] = nrm((POOL_DIM, D_MODEL), POOL_DIM ** -0.5)
        ffn(prefix)

    def sgu(prefix):
        p[prefix + "_norm1"] = gain(D_MODEL)
        p[prefix + "_sgu_w_in"] = nrm((D_MODEL, 2 * SGU_DIM), D_MODEL ** -0.5)
        p[prefix + "_sgu_v_norm_w"] = gain(SGU_DIM)
        p[prefix + "_sgu_v_norm_b"] = nrm((SGU_DIM,), 0.02)
        p[prefix + "_sgu_w_spatial"] = nrm((SGU_HEADS, SGU_CHUNK, SGU_CHUNK), SGU_CHUNK ** -0.5)
        p[prefix + "_sgu_b_spatial"] = 1.0 + nrm((SGU_HEADS, SGU_CHUNK), 0.02)
        p[prefix + "_sgu_w_out"] = nrm((SGU_DIM, D_MODEL), SGU_DIM ** -0.5)
        ffn(prefix)

    def gla(prefix):
        p[prefix + "_norm1"] = gain(D_MODEL)
        p[prefix + "_gla_w_in"] = nrm((D_MODEL, GLA_IN_DIM), D_MODEL ** -0.5)
        p[prefix + "_gla_w_gk_up"] = nrm((GLA_GATE_RANK, GLA_KEY_DIM), GLA_GATE_RANK ** -0.5)
        p[prefix + "_gla_b_gk"] = nrm((GLA_KEY_DIM,), 0.1)
        p[prefix + "_gla_out_norm_w"] = gain(GLA_VALUE_DIM)
        p[prefix + "_gla_w_out"] = nrm((GLA_VALUE_DIM, D_MODEL), GLA_VALUE_DIM ** -0.5)
        ffn(prefix)

    pool("l0")
    sgu("l1")
    gla("l2")
    pool("l3")
    p["final_norm_w"] = gain(D_MODEL)
    return p


def reference(x,
              l0_norm1, l0_pool_w_in, l0_pool_w_group, l0_pool_scale, l0_pool_w_out,
              l0_norm2, l0_ffn_w_gate, l0_ffn_w_up, l0_ffn_w_down,
              l1_norm1, l1_sgu_w_in, l1_sgu_v_norm_w, l1_sgu_v_norm_b, l1_sgu_w_spatial,
              l1_sgu_b_spatial, l1_sgu_w_out,
              l1_norm2, l1_ffn_w_gate, l1_ffn_w_up, l1_ffn_w_down,
              l2_norm1, l2_gla_w_in, l2_gla_w_gk_up, l2_gla_b_gk, l2_gla_out_norm_w, l2_gla_w_out,
              l2_norm2, l2_ffn_w_gate, l2_ffn_w_up, l2_ffn_w_down,
              l3_norm1, l3_pool_w_in, l3_pool_w_group, l3_pool_scale, l3_pool_w_out,
              l3_norm2, l3_ffn_w_gate, l3_ffn_w_up, l3_ffn_w_down,
              final_norm_w):
    norm1 = (l0_norm1, l1_norm1, l2_norm1, l3_norm1)
    mixer_params = (
        (l0_pool_w_in, l0_pool_w_group, l0_pool_scale, l0_pool_w_out),
        (l1_sgu_w_in, l1_sgu_v_norm_w, l1_sgu_v_norm_b, l1_sgu_w_spatial, l1_sgu_b_spatial, l1_sgu_w_out),
        (l2_gla_w_in, l2_gla_w_gk_up, l2_gla_b_gk, l2_gla_out_norm_w, l2_gla_w_out),
        (l3_pool_w_in, l3_pool_w_group, l3_pool_scale, l3_pool_w_out),
    )
    norm2 = (l0_norm2, l1_norm2, l2_norm2, l3_norm2)
    ffn_params = (
        (l0_ffn_w_gate, l0_ffn_w_up, l0_ffn_w_down),
        (l1_ffn_w_gate, l1_ffn_w_up, l1_ffn_w_down),
        (l2_ffn_w_gate, l2_ffn_w_up, l2_ffn_w_down),
        (l3_ffn_w_gate, l3_ffn_w_up, l3_ffn_w_down),
    )
    for i in range(DEPTH):
        mixer = MIXERS[i % N_MIXERS]
        x = x + mixer(rms_norm(x, norm1[i]), *mixer_params[i])
        x = x + swiglu_ffn(rms_norm(x, norm2[i]), *ffn_params[i])
    return rms_norm(x, final_norm_w)
```

```python
import functools

import jax
import jax.numpy as jnp
from jax import lax
from jax.experimental import pallas as pl
from jax.experimental.pallas import tpu as pltpu

F32 = jnp.float32
BF16 = jnp.bfloat16

EPS = 1e-6
POOL_WINDOWS = (2, 4, 8, 16)
POOL_HALO = 24
SGU_CHUNK = 128
GLA_HEADS = 4
GLA_CHUNK = 64
GLA_GATE_TEMP = 16.0
LANES = 128
VMEM_LIMIT = 52 * 1024 * 1024


def _params(*semantics):
    return pltpu.CompilerParams(dimension_semantics=semantics,
                                vmem_limit_bytes=VMEM_LIMIT)


def _tile(n, want):
    t = min(n, want)
    assert n % t == 0, (n, want)
    return t


def _rms_norm(x, w):
    ms = jnp.mean(x * x, axis=-1, keepdims=True)
    return x * lax.rsqrt(ms + EPS) * w


def _dot(a, b):
    return jnp.dot(a, b, preferred_element_type=F32)


def _const_spec(shape):
    return pl.BlockSpec(shape, lambda *_: (0,) * len(shape))


def _norm_matmul_kernel(x_ref, nw_ref, w_ref, o_ref, h_ref, *, gelu):
    @pl.when(pl.program_id(1) == 0)
    def _():
        h_ref[...] = _rms_norm(x_ref[...], nw_ref[...]).astype(BF16)

    z = _dot(h_ref[...], w_ref[...])
    if gelu:
        z = 0.5 * z * (1.0 + lax.erf(z * (2.0 ** -0.5)))
    o_ref[...] = z.astype(o_ref.dtype)


def _norm_matmul(x, norm_w, w, *, out_dtype, gelu=False, tm=512, tn=1024):
    t, d = x.shape
    n = w.shape[1]
    tm, tn = _tile(t, tm), _tile(n, tn)
    return pl.pallas_call(
        functools.partial(_norm_matmul_kernel, gelu=gelu),
        grid=(t // tm, n // tn),
        in_specs=[pl.BlockSpec((tm, d), lambda i, j: (i, 0)),
                  _const_spec((1, d)),
                  pl.BlockSpec((d, tn), lambda i, j: (0, j))],
        out_specs=pl.BlockSpec((tm, tn), lambda i, j: (i, j)),
        out_shape=jax.ShapeDtypeStruct((t, n), out_dtype),
        scratch_shapes=[pltpu.VMEM((tm, d), BF16)],
        compiler_params=_params("parallel", "arbitrary"),
        name="norm_matmul",
    )(x, norm_w.reshape(1, d), w)


def _matmul_residual_kernel(a_ref, w_ref, x_ref, o_ref):
    o_ref[...] = x_ref[...] + _dot(a_ref[...], w_ref[...])


def _matmul_residual(a, w, x, *, tm=512):
    t, k = a.shape
    d = w.shape[1]
    tm = _tile(t, tm)
    return pl.pallas_call(
        _matmul_residual_kernel,
        grid=(t // tm,),
        in_specs=[pl.BlockSpec((tm, k), lambda i: (i, 0)),
                  _const_spec((k, d)),
                  pl.BlockSpec((tm, d), lambda i: (i, 0))],
        out_specs=pl.BlockSpec((tm, d), lambda i: (i, 0)),
        out_shape=jax.ShapeDtypeStruct((t, d), F32),
        compiler_params=_params("parallel"),
        name="matmul_residual",
    )(a, w, x)


def _ffn_kernel(x_ref, nw_ref, wg_ref, wu_ref, wd_ref, o_ref, h_ref):
    @pl.when(pl.program_id(1) == 0)
    def _():
        x = x_ref[...]
        h_ref[...] = _rms_norm(x, nw_ref[...]).astype(BF16)
        o_ref[...] = x

    h = h_ref[...]
    g = _dot(h, wg_ref[...])
    u = _dot(h, wu_ref[...])
    a = (g * jax.nn.sigmoid(g) * u).astype(BF16)
    o_ref[...] += _dot(a, wd_ref[...])


def _ffn(x, norm_w, w_gate, w_up, w_down, *, tm=512, tf=512):
    t, d = x.shape
    f = w_gate.shape[1]
    tm, tf = _tile(t, tm), _tile(f, tf)
    return pl.pallas_call(
        _ffn_kernel,
        grid=(t // tm, f // tf),
        in_specs=[pl.BlockSpec((tm, d), lambda i, j: (i, 0)),
                  _const_spec((1, d)),
                  pl.BlockSpec((d, tf), lambda i, j: (0, j)),
                  pl.BlockSpec((d, tf), lambda i, j: (0, j)),
                  pl.BlockSpec((tf, d), lambda i, j: (j, 0))],
        out_specs=pl.BlockSpec((tm, d), lambda i, j: (i, 0)),
        out_shape=jax.ShapeDtypeStruct((t, d), F32),
        scratch_shapes=[pltpu.VMEM((tm, d), BF16)],
        compiler_params=_params("parallel", "arbitrary"),
        name="ffn",
    )(x, norm_w.reshape(1, d), w_gate, w_up, w_down)


def _pool_mid_kernel(z_ref, wg_ref, sc_ref, y_ref, zbuf, *, tiles_per_seq):
    tm = z_ref.shape[0]
    cg = wg_ref.shape[1]
    pos = pl.program_id(0) % tiles_per_seq
    halo = POOL_HALO

    @pl.when(pos == 0)
    def _():
        zbuf[0:halo, :] = jnp.zeros((halo, zbuf.shape[1]), F32)

    @pl.when(pos != 0)
    def _():
        zbuf[0:halo, :] = zbuf[tm:tm + halo, :]

    zbuf[halo:halo + tm, :] = z_ref[...]

    ext = tm + 16
    t_idx = pos * tm + lax.broadcasted_iota(jnp.int32, (tm, cg), 0)
    for g, w in enumerate(POOL_WINDOWS):
        cols = slice(g * cg, (g + 1) * cg)
        s = zbuf[8:8 + ext, cols]
        z = s[16:, :]
        shift = 1
        while shift < w:
            s = s + pltpu.roll(s, shift, 0)
            shift *= 2
        count = jnp.minimum(t_idx + 1, w).astype(F32)
        p = s[16:, :] / count - z
        y = _dot(p.astype(BF16), wg_ref[g]) * sc_ref[:, cols]
        y_ref[:, cols] = y.astype(y_ref.dtype)


def _pool_mid(z, w_group, scale, *, seq, tm=512):
    t, p = z.shape
    g, cg, _ = w_group.shape
    tm = _tile(seq, tm)
    return pl.pallas_call(
        functools.partial(_pool_mid_kernel, tiles_per_seq=seq // tm),
        grid=(t // tm,),
        in_specs=[pl.BlockSpec((tm, p), lambda i: (i, 0)),
                  _const_spec((g, cg, cg)),
                  _const_spec((1, p))],
        out_specs=pl.BlockSpec((tm, p), lambda i: (i, 0)),
        out_shape=jax.ShapeDtypeStruct((t, p), BF16),
        scratch_shapes=[pltpu.VMEM((tm + POOL_HALO, p), F32)],
        compiler_params=_params("arbitrary"),
        name="pool_mid",
    )(z, w_group, scale.reshape(1, p))


def _sgu_mid_kernel(u_ref, v_ref, lnw_ref, lnb_ref, ws_ref, bs_ref, o_ref):
    tm, dim = u_ref.shape
    heads = ws_ref.shape[0]
    hd = dim // heads
    c = SGU_CHUNK

    v = v_ref[...].astype(F32)
    mu = jnp.mean(v, axis=-1, keepdims=True)
    vc = v - mu
    var = jnp.mean(vc * vc, axis=-1, keepdims=True)
    vn = (vc * lax.rsqrt(var + EPS) * lnw_ref[...] + lnb_ref[...]).astype(BF16)

    row = lax.broadcasted_iota(jnp.int32, (c, c), 0)
    col = lax.broadcasted_iota(jnp.int32, (c, c), 1)
    causal = col <= row
    for h in range(heads):
        wm = jnp.where(causal, ws_ref[h], 0.0).astype(BF16)
        bias = bs_ref[:, h:h + 1]
        cols = slice(h * hd, (h + 1) * hd)
        for n in range(tm // c):
            rows = slice(n * c, (n + 1) * c)
            gate = _dot(wm, vn[rows, cols]) + bias
            o_ref[rows, cols] = (u_ref[rows, cols].astype(F32) * gate).astype(o_ref.dtype)


def _sgu_mid(z, ln_w, ln_b, w_spatial, b_spatial, *, tm=512):
    t, two_dim = z.shape
    dim = two_dim // 2
    heads, c, _ = w_spatial.shape
    tm = _tile(t, tm)
    return pl.pallas_call(
        _sgu_mid_kernel,
        grid=(t // tm,),
        in_specs=[pl.BlockSpec((tm, dim), lambda i: (i, 0)),
                  pl.BlockSpec((tm, dim), lambda i: (i, 1)),
                  _const_spec((1, dim)),
                  _const_spec((1, dim)),
                  _const_spec((heads, c, c)),
                  _const_spec((c, heads))],
        out_specs=pl.BlockSpec((tm, dim), lambda i: (i, 0)),
        out_shape=jax.ShapeDtypeStruct((t, dim), BF16),
        compiler_params=_params("parallel"),
        name="sgu_mid",
    )(z, z, ln_w.reshape(1, dim), ln_b.reshape(1, dim), w_spatial, b_spatial.T)


def _gla_in_kernel(x_ref, nw_ref, w_ref, wlo_ref, wup_ref, bgk_ref, z_ref, g_ref, h_ref):
    @pl.when(pl.program_id(1) == 0)
    def _():
        h = _rms_norm(x_ref[...], nw_ref[...]).astype(BF16)
        h_ref[...] = h
        low = _dot(h, wlo_ref[...]).astype(BF16)
        gk = _dot(low, wup_ref[...]) + bgk_ref[...]
        g_ref[...] = jax.nn.log_sigmoid(gk) / GLA_GATE_TEMP

    z_ref[...] = _dot(h_ref[...], w_ref[...]).astype(z_ref.dtype)


def _gla_in(x, norm_w, w_main, w_low, w_up, b_gk, *, tm=512, tn=1024):
    t, d = x.shape
    n = w_main.shape[1]
    rank, dk = w_up.shape
    tm, tn = _tile(t, tm), _tile(n, tn)
    return pl.pallas_call(
        _gla_in_kernel,
        grid=(t // tm, n // tn),
        in_specs=[pl.BlockSpec((tm, d), lambda i, j: (i, 0)),
                  _const_spec((1, d)),
                  pl.BlockSpec((d, tn), lambda i, j: (0, j)),
                  _const_spec((d, rank)),
                  _const_spec((rank, dk)),
                  _const_spec((1, dk))],
        out_specs=[pl.BlockSpec((tm, tn), lambda i, j: (i, j)),
                   pl.BlockSpec((tm, dk), lambda i, j: (i, 0))],
        out_shape=[jax.ShapeDtypeStruct((t, n), BF16),
                   jax.ShapeDtypeStruct((t, dk), F32)],
        scratch_shapes=[pltpu.VMEM((tm, d), BF16)],
        compiler_params=_params("parallel", "arbitrary"),
        name="gla_in",
    )(x, norm_w.reshape(1, d), w_main, w_low, w_up, b_gk.reshape(1, dk))


def _gla_mid_kernel(q_ref, k_ref, v_ref, r_ref, g_ref, onw_ref, o_ref, st_ref,
                    *, tiles_per_seq):
    tm, dkt = q_ref.shape
    dvt = v_ref.shape[1]
    hk, hv = dkt // GLA_HEADS, dvt // GLA_HEADS
    c = GLA_CHUNK

    @pl.when(pl.program_id(0) % tiles_per_seq == 0)
    def _():
        st_ref[...] = jnp.zeros(st_ref.shape, F32)

    row_k = lax.broadcasted_iota(jnp.int32, (c, hk), 0)
    row = lax.broadcasted_iota(jnp.int32, (c, c), 0)
    col = lax.broadcasted_iota(jnp.int32, (c, c), 1)
    causal = col <= row
    q_scale = hk ** -0.5

    def chunk(n, carry):
        rows = pl.ds(pl.multiple_of(n * c, c), c)
        for h in range(GLA_HEADS):
            kc = slice(h * hk, (h + 1) * hk)
            vc = slice(h * hv, (h + 1) * hv)
            b = g_ref[rows, kc]
            shift = 1
            while shift < c:
                b = b + jnp.where(row_k >= shift, pltpu.roll(b, shift, 0), 0.0)
                shift *= 2
            b_last = b[c - 1:c, :]
            q = q_ref[rows, kc].astype(F32) * q_scale
            k = k_ref[rows, kc].astype(F32)
            v = v_ref[rows, vc]
            q_in = (q * jnp.exp(b)).astype(BF16)
            k_in = (k * jnp.exp(-b)).astype(BF16)
            k_dec = (k * jnp.exp(b_last - b)).astype(BF16)

            scores = lax.dot_general(q_in, k_in, (((1,), (1,)), ((), ())),
                                     preferred_element_type=F32)
            scores = jnp.where(causal, scores, 0.0).astype(BF16)
            st = st_ref[h]
            o = _dot(scores, v) + lax.dot_general(
                q_in, st.astype(BF16), (((1,), (1,)), ((), ())),
                preferred_element_type=F32)
            st_ref[h] = jnp.exp(b_last) * st + lax.dot_general(
                v, k_dec, (((0,), (0,)), ((), ())), preferred_element_type=F32)

            o = o * lax.rsqrt(jnp.mean(o * o, axis=-1, keepdims=True) + EPS)
            o = o * onw_ref[:, vc]
            r = r_ref[rows, vc].astype(F32)
            o_ref[rows, vc] = (o * (r * jax.nn.sigmoid(r))).astype(o_ref.dtype)
        return carry

    lax.fori_loop(0, tm // c, chunk, 0)


def _gla_mid(z, g, out_norm_w, *, seq, dk, dv, tm=512):
    t = z.shape[0]
    assert dv == 2 * dk
    tm = _tile(seq, tm)
    hk, hv = dk // GLA_HEADS, dv // GLA_HEADS
    return pl.pallas_call(
        functools.partial(_gla_mid_kernel, tiles_per_seq=seq // tm),
        grid=(t // tm,),
        in_specs=[pl.BlockSpec((tm, dk), lambda i: (i, 0)),
                  pl.BlockSpec((tm, dk), lambda i: (i, 1)),
                  pl.BlockSpec((tm, dv), lambda i: (i, 1)),
                  pl.BlockSpec((tm, dv), lambda i: (i, 2)),
                  pl.BlockSpec((tm, dk), lambda i: (i, 0)),
                  _const_spec((1, dv))],
        out_specs=pl.BlockSpec((tm, dv), lambda i: (i, 0)),
        out_shape=jax.ShapeDtypeStruct((t, dv), BF16),
        scratch_shapes=[pltpu.VMEM((GLA_HEADS, hv, hk), F32)],
        compiler_params=_params("arbitrary"),
        name="gla_mid",
    )(z, z, z, z, g, out_norm_w.reshape(1, dv))


def _pool_layer(x, seq, norm_w, w_in, w_group, scale, w_out):
    z = _norm_matmul(x, norm_w, w_in.astype(BF16), out_dtype=F32)
    y = _pool_mid(z, w_group.astype(BF16), scale, seq=seq)
    return _matmul_residual(y, w_out.astype(BF16), x)


def _sgu_layer(x, norm_w, w_in, ln_w, ln_b, w_spatial, b_spatial, w_out):
    z = _norm_matmul(x, norm_w, w_in.astype(BF16), out_dtype=BF16, gelu=True)
    a = _sgu_mid(z, ln_w, ln_b, w_spatial, b_spatial)
    return _matmul_residual(a, w_out.astype(BF16), x)


def _gla_layer(x, seq, norm_w, w_in, w_gk_up, b_gk, out_norm_w, w_out):
    rank, dk = w_gk_up.shape
    dv = out_norm_w.shape[0]
    n_main = 2 * dk + 2 * dv
    rank_pad = -(-rank // LANES) * LANES
    w_main = w_in[:, :n_main].astype(BF16)
    w_low = jnp.pad(w_in[:, n_main:], ((0, 0), (0, rank_pad - rank))).astype(BF16)
    w_up = jnp.pad(w_gk_up, ((0, rank_pad - rank), (0, 0))).astype(BF16)
    z, g = _gla_in(x, norm_w, w_main, w_low, w_up, b_gk)
    a = _gla_mid(z, g, out_norm_w, seq=seq, dk=dk, dv=dv)
    return _matmul_residual(a, w_out.astype(BF16), x)


def _ffn_layer(x, norm_w, w_gate, w_up, w_down):
    return _ffn(x, norm_w, w_gate.astype(BF16), w_up.astype(BF16), w_down.astype(BF16))


def _final_norm_kernel(x_ref, w_ref, o_ref):
    o_ref[...] = _rms_norm(x_ref[...], w_ref[...])


def _final_norm(x, w, *, tm=512):
    t, d = x.shape
    tm = _tile(t, tm)
    return pl.pallas_call(
        _final_norm_kernel,
        grid=(t // tm,),
        in_specs=[pl.BlockSpec((tm, d), lambda i: (i, 0)), _const_spec((1, d))],
        out_specs=pl.BlockSpec((tm, d), lambda i: (i, 0)),
        out_shape=jax.ShapeDtypeStruct((t, d), F32),
        compiler_params=_params("parallel"),
        name="final_norm",
    )(x, w.reshape(1, d))


def kernel(x, l0_norm1, l0_pool_w_in, l0_pool_w_group, l0_pool_scale, l0_pool_w_out, l0_norm2, l0_ffn_w_gate, l0_ffn_w_up, l0_ffn_w_down, l1_norm1, l1_sgu_w_in, l1_sgu_v_norm_w, l1_sgu_v_norm_b, l1_sgu_w_spatial, l1_sgu_b_spatial, l1_sgu_w_out, l1_norm2, l1_ffn_w_gate, l1_ffn_w_up, l1_ffn_w_down, l2_norm1, l2_gla_w_in, l2_gla_w_gk_up, l2_gla_b_gk, l2_gla_out_norm_w, l2_gla_w_out, l2_norm2, l2_ffn_w_gate, l2_ffn_w_up, l2_ffn_w_down, l3_norm1, l3_pool_w_in, l3_pool_w_group, l3_pool_scale, l3_pool_w_out, l3_norm2, l3_ffn_w_gate, l3_ffn_w_up, l3_ffn_w_down, final_norm_w):
    batch, seq, d = x.shape
    h = x.reshape(batch * seq, d)
    h = _pool_layer(h, seq, l0_norm1, l0_pool_w_in, l0_pool_w_group, l0_pool_scale, l0_pool_w_out)
    h = _ffn_layer(h, l0_norm2, l0_ffn_w_gate, l0_ffn_w_up, l0_ffn_w_down)
    h = _sgu_layer(h, l1_norm1, l1_sgu_w_in, l1_sgu_v_norm_w, l1_sgu_v_norm_b,
                   l1_sgu_w_spatial, l1_sgu_b_spatial, l1_sgu_w_out)
    h = _ffn_layer(h, l1_norm2, l1_ffn_w_gate, l1_ffn_w_up, l1_ffn_w_down)
    h = _gla_layer(h, seq, l2_norm1, l2_gla_w_in, l2_gla_w_gk_up, l2_gla_b_gk,
                   l2_gla_out_norm_w, l2_gla_w_out)
    h = _ffn_layer(h, l2_norm2, l2_ffn_w_gate, l2_ffn_w_up, l2_ffn_w_down)
    h = _pool_layer(h, seq, l3_norm1, l3_pool_w_in, l3_pool_w_group, l3_pool_scale, l3_pool_w_out)
    h = _ffn_layer(h, l3_norm2, l3_ffn_w_gate, l3_ffn_w_up, l3_ffn_w_down)
    return _final_norm(h, final_norm_w).reshape(batch, seq, d)
```

```python
import functools

import jax
import jax.numpy as jnp
from jax import lax
from jax.experimental import pallas as pl
from jax.experimental.pallas import tpu as pltpu

F32 = jnp.float32
BF16 = jnp.bfloat16

EPS = 1e-6
POOL_WINDOWS = (2, 4, 8, 16)
POOL_HALO = 24
SGU_CHUNK = 128
GLA_HEADS = 4
GLA_CHUNK = 64
GLA_GATE_TEMP = 16.0
LANES = 128
VMEM_LIMIT = 52 * 1024 * 1024


def _params(*semantics):
    return pltpu.CompilerParams(dimension_semantics=semantics,
                                vmem_limit_bytes=VMEM_LIMIT)


def _tile(n, want):
    t = min(n, want)
    assert n % t == 0, (n, want)
    return t


def _rms_norm(x, w):
    ms = jnp.mean(x * x, axis=-1, keepdims=True)
    return x * lax.rsqrt(ms + EPS) * w


def _dot(a, b):
    return jnp.dot(a, b, preferred_element_type=F32)


def _const_spec(shape):
    return pl.BlockSpec(shape, lambda *_: (0,) * len(shape))


def _resident_spec(shape):
    return pl.BlockSpec(shape, lambda *_: (0,) * len(shape),
                        pipeline_mode=pl.Buffered(1))


def _pool_layer_kernel(x_ref, nw_ref, win_ref, wg_ref, sc_ref, wout_ref, o_ref, zbuf,
                       *, tiles_per_seq, sub):
    tm = x_ref.shape[0]
    cg = wg_ref.shape[1]
    pos = pl.program_id(0) % tiles_per_seq
    halo = POOL_HALO

    @pl.when(pos == 0)
    def _():
        zbuf[0:halo, :] = jnp.zeros((halo, zbuf.shape[1]), F32)

    @pl.when(pos != 0)
    def _():
        zbuf[0:halo, :] = zbuf[tm:tm + halo, :]

    ext = sub + 16
    for r0 in range(0, tm, sub):
        x = x_ref[r0:r0 + sub, :]
        h = _rms_norm(x, nw_ref[...]).astype(BF16)
        t_idx = pos * tm + r0 + lax.broadcasted_iota(jnp.int32, (sub, cg), 0)
        ys = []
        for g, w in enumerate(POOL_WINDOWS):
            cols = slice(g * cg, (g + 1) * cg)
            z = _dot(h, win_ref[:, cols])
            zbuf[halo + r0:halo + r0 + sub, cols] = z
            s = zbuf[8 + r0:8 + r0 + ext, cols]
            shift = 1
            while shift < w:
                s = s + pltpu.roll(s, shift, 0)
                shift *= 2
            count = jnp.minimum(t_idx + 1, w).astype(F32)
            p = s[16:, :] / count - z
            y = _dot(p.astype(BF16), wg_ref[g]) * sc_ref[:, cols]
            ys.append(y.astype(BF16))
        y = jnp.concatenate(ys, axis=-1)
        o_ref[r0:r0 + sub, :] = x + _dot(y, wout_ref[...])


def _pool_layer(x, seq, norm_w, w_in, w_group, scale, w_out, *, tm=512, sub=256):
    t, d = x.shape
    p = w_in.shape[1]
    g, cg, _ = w_group.shape
    tm = _tile(seq, tm)
    sub = _tile(tm, sub)
    return pl.pallas_call(
        functools.partial(_pool_layer_kernel, tiles_per_seq=seq // tm, sub=sub),
        grid=(t // tm,),
        in_specs=[pl.BlockSpec((tm, d), lambda i: (i, 0)),
                  _resident_spec((1, d)),
                  _resident_spec((d, p)),
                  _resident_spec((g, cg, cg)),
                  _resident_spec((1, p)),
                  _resident_spec((p, d))],
        out_specs=pl.BlockSpec((tm, d), lambda i: (i, 0)),
        out_shape=jax.ShapeDtypeStruct((t, d), F32),
        scratch_shapes=[pltpu.VMEM((tm + POOL_HALO, p), F32)],
        compiler_params=_params("arbitrary"),
        name="pool_layer",
    )(x, norm_w.reshape(1, d), w_in.astype(BF16), w_group.astype(BF16),
      scale.reshape(1, p), w_out.astype(BF16))


def _ffn_kernel(x_ref, nw_ref, wg_ref, wu_ref, wd_ref, *rest):
    fnw_ref = rest[0] if len(rest) == 3 else None
    o_ref, h_ref = rest[-2:]
    j = pl.program_id(1)

    @pl.when(j == 0)
    def _():
        x = x_ref[...]
        h_ref[...] = _rms_norm(x, nw_ref[...]).astype(BF16)
        o_ref[...] = x

    h = h_ref[...]
    g = _dot(h, wg_ref[...])
    u = _dot(h, wu_ref[...])
    a = (g * jax.nn.sigmoid(g) * u).astype(BF16)
    o_ref[...] += _dot(a, wd_ref[...])

    if fnw_ref is not None:
        @pl.when(j == pl.num_programs(1) - 1)
        def _():
            o_ref[...] = _rms_norm(o_ref[...], fnw_ref[...])


def _ffn(x, norm_w, w_gate, w_up, w_down, final_norm_w=None, *, tm=512, tf=512):
    t, d = x.shape
    f = w_gate.shape[1]
    tm, tf = _tile(t, tm), _tile(f, tf)
    in_specs = [pl.BlockSpec((tm, d), lambda i, j: (i, 0)),
                _const_spec((1, d)),
                pl.BlockSpec((d, tf), lambda i, j: (0, j)),
                pl.BlockSpec((d, tf), lambda i, j: (0, j)),
                pl.BlockSpec((tf, d), lambda i, j: (j, 0))]
    args = [x, norm_w.reshape(1, d), w_gate.astype(BF16), w_up.astype(BF16),
            w_down.astype(BF16)]
    if final_norm_w is not None:
        in_specs.append(_const_spec((1, d)))
        args.append(final_norm_w.reshape(1, d))
    return pl.pallas_call(
        _ffn_kernel,
        grid=(t // tm, f // tf),
        in_specs=in_specs,
        out_specs=pl.BlockSpec((tm, d), lambda i, j: (i, 0)),
        out_shape=jax.ShapeDtypeStruct((t, d), F32),
        scratch_shapes=[pltpu.VMEM((tm, d), BF16)],
        compiler_params=_params("parallel", "arbitrary"),
        name="ffn",
    )(*args)


def _gelu(z):
    return 0.5 * z * (1.0 + lax.erf(z * (2.0 ** -0.5)))


def _sgu_layer_kernel(x_ref, nw_ref, win_ref, lnw_ref, lnb_ref, ws_ref, bs_ref, wout_ref,
                      o_ref, *, sub):
    tm = x_ref.shape[0]
    dim = lnw_ref.shape[1]
    heads = ws_ref.shape[0]
    hd = dim // heads
    c = SGU_CHUNK

    row = lax.broadcasted_iota(jnp.int32, (c, c), 0)
    col = lax.broadcasted_iota(jnp.int32, (c, c), 1)
    causal = col <= row
    wms = [jnp.where(causal, ws_ref[h], 0.0).astype(BF16) for h in range(heads)]

    for r0 in range(0, tm, sub):
        x = x_ref[r0:r0 + sub, :]
        h = _rms_norm(x, nw_ref[...]).astype(BF16)
        v = _gelu(_dot(h, win_ref[:, dim:]))
        mu = jnp.mean(v, axis=-1, keepdims=True)
        vc = v - mu
        var = jnp.mean(vc * vc, axis=-1, keepdims=True)
        vn = (vc * lax.rsqrt(var + EPS) * lnw_ref[...] + lnb_ref[...]).astype(BF16)
        parts = []
        for hh in range(heads):
            cols = slice(hh * hd, (hh + 1) * hd)
            u = _gelu(_dot(h, win_ref[:, cols]))
            bias = bs_ref[:, hh:hh + 1]
            gate = jnp.concatenate(
                [_dot(wms[hh], vn[n * c:(n + 1) * c, cols]) + bias for n in range(sub // c)],
                axis=0)
            parts.append((u * gate).astype(BF16))
        a = jnp.concatenate(parts, axis=-1)
        o_ref[r0:r0 + sub, :] = x + _dot(a, wout_ref[...])


def _sgu_layer(x, norm_w, w_in, ln_w, ln_b, w_spatial, b_spatial, w_out, *, tm=512, sub=256):
    t, d = x.shape
    dim = w_out.shape[0]
    heads, c, _ = w_spatial.shape
    tm = _tile(t, tm)
    sub = _tile(tm, sub)
    return pl.pallas_call(
        functools.partial(_sgu_layer_kernel, sub=sub),
        grid=(t // tm,),
        in_specs=[pl.BlockSpec((tm, d), lambda i: (i, 0)),
                  _resident_spec((1, d)),
                  _resident_spec((d, 2 * dim)),
                  _resident_spec((1, dim)),
                  _resident_spec((1, dim)),
                  _resident_spec((heads, c, c)),
                  _resident_spec((c, heads)),
                  _resident_spec((dim, d))],
        out_specs=pl.BlockSpec((tm, d), lambda i: (i, 0)),
        out_shape=jax.ShapeDtypeStruct((t, d), F32),
        compiler_params=_params("parallel"),
        name="sgu_layer",
    )(x, norm_w.reshape(1, d), w_in.astype(BF16), ln_w.reshape(1, dim), ln_b.reshape(1, dim),
      w_spatial, b_spatial.T, w_out.astype(BF16))


def _gla_layer_kernel(x_ref, nw_ref, w_ref, wlo_ref, wup_ref, bgk_ref, onw_ref, wout_ref,
                      o_ref, st_ref, *, tiles_per_seq):
    tm = x_ref.shape[0]
    dkt = wup_ref.shape[1]
    dvt = onw_ref.shape[1]
    hk, hv = dkt // GLA_HEADS, dvt // GLA_HEADS
    c = GLA_CHUNK

    @pl.when(pl.program_id(0) % tiles_per_seq == 0)
    def _():
        st_ref[...] = jnp.zeros(st_ref.shape, F32)

    row_k = lax.broadcasted_iota(jnp.int32, (c, hk), 0)
    row = lax.broadcasted_iota(jnp.int32, (c, c), 0)
    col = lax.broadcasted_iota(jnp.int32, (c, c), 1)
    causal = col <= row
    nt = (((1,), (1,)), ((), ()))
    tn = (((0,), (0,)), ((), ()))

    x = x_ref[...]
    hn = _rms_norm(x, nw_ref[...]).astype(BF16)
    low = _dot(hn, wlo_ref[...]).astype(BF16)

    def proj_piece(h, piece, acc):
        kc = slice(h * hk, (h + 1) * hk)
        vc = slice(h * hv, (h + 1) * hv)
        if piece == 0:
            acc["q"] = _dot(hn, w_ref[:, h * hk:(h + 1) * hk]) * (hk ** -0.5)
            acc["g"] = jax.nn.log_sigmoid(
                _dot(low, wup_ref[:, kc]) + bgk_ref[:, kc]) / GLA_GATE_TEMP
        elif piece == 1:
            acc["k"] = _dot(hn, w_ref[:, dkt + h * hk:dkt + (h + 1) * hk])
        elif piece == 2:
            acc["v"] = _dot(hn, w_ref[:, 2 * dkt + h * hv:2 * dkt + (h + 1) * hv]).astype(BF16)
        else:
            r = _dot(hn, w_ref[:, 2 * dkt + dvt + h * hv:2 * dkt + dvt + (h + 1) * hv])
            acc["gate"] = r * jax.nn.sigmoid(r) * onw_ref[:, vc]

    n_chunks = tm // c
    cur = {}
    for piece in range(4):
        proj_piece(0, piece, cur)
    out = x
    for h in range(GLA_HEADS):
        q_all, k_all, v_all, g_all, gate_all = (cur[n] for n in ("q", "k", "v", "g", "gate"))
        nxt = {}
        outs = []
        for n in range(n_chunks):
            if h + 1 < GLA_HEADS:
                for piece in range(n * 4 // n_chunks, (n + 1) * 4 // n_chunks):
                    proj_piece(h + 1, piece, nxt)
            rows = slice(n * c, (n + 1) * c)
            b = g_all[rows]
            shift = 1
            while shift < c:
                b = b + jnp.where(row_k >= shift, pltpu.roll(b, shift, 0), 0.0)
                shift *= 2
            b_last = b[c - 1:c, :]
            k = k_all[rows]
            v = v_all[rows]
            q_in = (q_all[rows] * jnp.exp(b)).astype(BF16)
            k_in = (k * jnp.exp(-b)).astype(BF16)
            k_dec = (k * jnp.exp(b_last - b)).astype(BF16)
            scores = lax.dot_general(q_in, k_in, nt, preferred_element_type=F32)
            scores = jnp.where(causal, scores, 0.0).astype(BF16)
            st = st_ref[h]
            o = _dot(scores, v) + lax.dot_general(q_in, st.astype(BF16), nt,
                                                  preferred_element_type=F32)
            st_ref[h] = jnp.exp(b_last) * st + lax.dot_general(
                v, k_dec, tn, preferred_element_type=F32)
            o = o * lax.rsqrt(jnp.mean(o * o, axis=-1, keepdims=True) + EPS)
            outs.append((o * gate_all[rows]).astype(BF16))
        out = out + _dot(jnp.concatenate(outs, axis=0), wout_ref[h * hv:(h + 1) * hv, :])
        cur = nxt
    o_ref[...] = out


def _gla_layer(x, seq, norm_w, w_in, w_gk_up, b_gk, out_norm_w, w_out, *, tm=256):
    t, d = x.shape
    rank, dk = w_gk_up.shape
    dv = out_norm_w.shape[0]
    tm = _tile(seq, tm)
    hk, hv = dk // GLA_HEADS, dv // GLA_HEADS
    n_main = 2 * dk + 2 * dv
    rank_pad = -(-rank // LANES) * LANES
    w_main = w_in[:, :n_main].astype(BF16)
    w_low = jnp.pad(w_in[:, n_main:], ((0, 0), (0, rank_pad - rank))).astype(BF16)
    w_up = jnp.pad(w_gk_up, ((0, rank_pad - rank), (0, 0))).astype(BF16)
    return pl.pallas_call(
        functools.partial(_gla_layer_kernel, tiles_per_seq=seq // tm),
        grid=(t // tm,),
        in_specs=[pl.BlockSpec((tm, d), lambda i: (i, 0)),
                  _resident_spec((1, d)),
                  _resident_spec((d, n_main)),
                  _resident_spec((d, rank_pad)),
                  _resident_spec((rank_pad, dk)),
                  _resident_spec((1, dk)),
                  _resident_spec((1, dv)),
                  _resident_spec((dv, d))],
        out_specs=pl.BlockSpec((tm, d), lambda i: (i, 0)),
        out_shape=jax.ShapeDtypeStruct((t, d), F32),
        scratch_shapes=[pltpu.VMEM((GLA_HEADS, hv, hk), F32)],
        compiler_params=_params("arbitrary"),
        name="gla_layer",
    )(x, norm_w.reshape(1, d), w_main, w_low, w_up, b_gk.reshape(1, dk),
      out_norm_w.reshape(1, dv), w_out.astype(BF16))


def kernel(x, l0_norm1, l0_pool_w_in, l0_pool_w_group, l0_pool_scale, l0_pool_w_out, l0_norm2, l0_ffn_w_gate, l0_ffn_w_up, l0_ffn_w_down, l1_norm1, l1_sgu_w_in, l1_sgu_v_norm_w, l1_sgu_v_norm_b, l1_sgu_w_spatial, l1_sgu_b_spatial, l1_sgu_w_out, l1_norm2, l1_ffn_w_gate, l1_ffn_w_up, l1_ffn_w_down, l2_norm1, l2_gla_w_in, l2_gla_w_gk_up, l2_gla_b_gk, l2_gla_out_norm_w, l2_gla_w_out, l2_norm2, l2_ffn_w_gate, l2_ffn_w_up, l2_ffn_w_down, l3_norm1, l3_pool_w_in, l3_pool_w_group, l3_pool_scale, l3_pool_w_out, l3_norm2, l3_ffn_w_gate, l3_ffn_w_up, l3_ffn_w_down, final_norm_w):
    batch, seq, d = x.shape
    h = x.reshape(batch * seq, d)
    h = _pool_layer(h, seq, l0_norm1, l0_pool_w_in, l0_pool_w_group, l0_pool_scale, l0_pool_w_out)
    h = _ffn(h, l0_norm2, l0_ffn_w_gate, l0_ffn_w_up, l0_ffn_w_down)
    h = _sgu_layer(h, l1_norm1, l1_sgu_w_in, l1_sgu_v_norm_w, l1_sgu_v_norm_b,
                   l1_sgu_w_spatial, l1_sgu_b_spatial, l1_sgu_w_out)
    h = _ffn(h, l1_norm2, l1_ffn_w_gate, l1_ffn_w_up, l1_ffn_w_down)
    h = _gla_layer(h, seq, l2_norm1, l2_gla_w_in, l2_gla_w_gk_up, l2_gla_b_gk,
                   l2_gla_out_norm_w, l2_gla_w_out)
    h = _ffn(h, l2_norm2, l2_ffn_w_gate, l2_ffn_w_up, l2_ffn_w_down)
    h = _pool_layer(h, seq, l3_norm1, l3_pool_w_in, l3_pool_w_group, l3_pool_scale, l3_pool_w_out)
    h = _ffn(h, l3_norm2, l3_ffn_w_gate, l3_ffn_w_up, l3_ffn_w_down, final_norm_w)
    return h.reshape(batch, seq, d)
```

```python
import functools

import jax
import jax.numpy as jnp
from jax import lax
from jax.experimental import pallas as pl
from jax.experimental.pallas import tpu as pltpu

F32 = jnp.float32
BF16 = jnp.bfloat16

EPS = 1e-6
POOL_WINDOWS = (2, 4, 8, 16)
POOL_HALO = 24
SGU_CHUNK = 128
GLA_HEADS = 4
GLA_CHUNK = 64
GLA_GATE_TEMP = 16.0
LANES = 128
VMEM_LIMIT = 52 * 1024 * 1024


def _params(*semantics):
    return pltpu.CompilerParams(dimension_semantics=semantics,
                                vmem_limit_bytes=VMEM_LIMIT)


def _tile(n, want):
    t = min(n, want)
    assert n % t == 0, (n, want)
    return t


def _rms_norm(x, w):
    ms = jnp.mean(x * x, axis=-1, keepdims=True)
    return x * lax.rsqrt(ms + EPS) * w


def _dot(a, b):
    return jnp.dot(a, b, preferred_element_type=F32)


def _const_spec(shape):
    return pl.BlockSpec(shape, lambda *_: (0,) * len(shape))


def _resident_spec(shape):
    return pl.BlockSpec(shape, lambda *_: (0,) * len(shape),
                        pipeline_mode=pl.Buffered(1))


def _pool_layer_kernel(x_ref, nw_ref, win_ref, wg_ref, sc_ref, wout_ref, o_ref, zbuf,
                       *, tiles_per_seq, sub):
    tm = x_ref.shape[0]
    cg = wg_ref.shape[1]
    pos = pl.program_id(0) % tiles_per_seq
    halo = POOL_HALO

    @pl.when(pos == 0)
    def _():
        zbuf[0:halo, :] = jnp.zeros((halo, zbuf.shape[1]), F32)

    @pl.when(pos != 0)
    def _():
        zbuf[0:halo, :] = zbuf[tm:tm + halo, :]

    ext = sub + 16
    for r0 in range(0, tm, sub):
        x = x_ref[r0:r0 + sub, :]
        h = _rms_norm(x, nw_ref[...]).astype(BF16)
        t_idx = pos * tm + r0 + lax.broadcasted_iota(jnp.int32, (sub, cg), 0)
        ys = []
        for g, w in enumerate(POOL_WINDOWS):
            cols = slice(g * cg, (g + 1) * cg)
            z = _dot(h, win_ref[:, cols])
            zbuf[halo + r0:halo + r0 + sub, cols] = z
            s = zbuf[8 + r0:8 + r0 + ext, cols]
            shift = 1
            while shift < w:
                s = s + pltpu.roll(s, shift, 0)
                shift *= 2
            count = jnp.minimum(t_idx + 1, w).astype(F32)
            p = s[16:, :] / count - z
            y = _dot(p.astype(BF16), wg_ref[g]) * sc_ref[:, cols]
            ys.append(y.astype(BF16))
        y = jnp.concatenate(ys, axis=-1)
        o_ref[r0:r0 + sub, :] = x + _dot(y, wout_ref[...])


def _pool_layer(x, seq, norm_w, w_in, w_group, scale, w_out, *, tm=512, sub=256):
    t, d = x.shape
    p = w_in.shape[1]
    g, cg, _ = w_group.shape
    tm = _tile(seq, tm)
    sub = _tile(tm, sub)
    return pl.pallas_call(
        functools.partial(_pool_layer_kernel, tiles_per_seq=seq // tm, sub=sub),
        grid=(t // tm,),
        in_specs=[pl.BlockSpec((tm, d), lambda i: (i, 0)),
                  _resident_spec((1, d)),
                  _resident_spec((d, p)),
                  _resident_spec((g, cg, cg)),
                  _resident_spec((1, p)),
                  _resident_spec((p, d))],
        out_specs=pl.BlockSpec((tm, d), lambda i: (i, 0)),
        out_shape=jax.ShapeDtypeStruct((t, d), F32),
        scratch_shapes=[pltpu.VMEM((tm + POOL_HALO, p), F32)],
        compiler_params=_params("arbitrary"),
        name="pool_layer",
    )(x, norm_w.reshape(1, d), w_in.astype(BF16), w_group.astype(BF16),
      scale.reshape(1, p), w_out.astype(BF16))


FFN_NORM_STEPS = 4


def _ffn_kernel(x_hbm, nw_ref, wg_ref, wu_ref, wd_ref, *rest):
    fnw_ref = rest[0] if len(rest) == 6 else None
    o_hbm, acc, hbuf, in_sem, out_sem = rest[-5:]
    i, j = pl.program_id(0), pl.program_id(1)
    nt, nj = pl.num_programs(0), pl.num_programs(1)
    slot = i % 2
    other = 1 - slot
    tm = acc.shape[1]
    rq = tm // FFN_NORM_STEPS

    def x_copy(tile, s):
        return pltpu.make_async_copy(x_hbm.at[pl.ds(tile * tm, tm), :], acc.at[s], in_sem.at[s])

    def o_copy(tile, s):
        return pltpu.make_async_copy(acc.at[s], o_hbm.at[pl.ds(tile * tm, tm), :], out_sem.at[s])

    @pl.when((i == 0) & (j == 0))
    def _():
        first = x_copy(0, 0)
        first.start()
        first.wait()
        hbuf[0] = _rms_norm(acc[0], nw_ref[...]).astype(BF16)

    @pl.when((j == 1) & (i >= 1))
    def _():
        o_copy(i - 1, other).wait()

    @pl.when((j == 1) & (i + 1 < nt))
    def _():
        x_copy(i + 1, other).start()

    q = j - (nj - FFN_NORM_STEPS)
    norm_next = (q >= 0) & (i + 1 < nt)

    @pl.when((q == 0) & (i + 1 < nt))
    def _():
        x_copy(i + 1, other).wait()

    def step(with_norm):
        h = hbuf[slot]
        g = _dot(h, wg_ref[...])
        u = _dot(h, wu_ref[...])
        a = (g * jax.nn.sigmoid(g) * u).astype(BF16)
        acc[slot] += _dot(a, wd_ref[...])
        if with_norm:
            rows = pl.ds(pl.multiple_of(q * rq, rq), rq)
            hbuf[other, rows, :] = _rms_norm(acc[other, rows, :], nw_ref[...]).astype(BF16)

    pl.when(norm_next)(functools.partial(step, True))
    pl.when(jnp.logical_not(norm_next))(functools.partial(step, False))

    @pl.when(j == nj - 1)
    def _():
        if fnw_ref is not None:
            acc[slot] = _rms_norm(acc[slot], fnw_ref[...])
        o_copy(i, slot).start()

    @pl.when((j == nj - 1) & (i == nt - 1))
    def _():
        o_copy(i, slot).wait()


def _ffn(x, norm_w, w_gate, w_up, w_down, final_norm_w=None, *, tm=1024, tf=512):
    t, d = x.shape
    f = w_gate.shape[1]
    tm, tf = _tile(t, tm), _tile(f, tf)
    nt, nj = t // tm, f // tf
    assert nt >= 2 and nj >= FFN_NORM_STEPS + 2 and tm % FFN_NORM_STEPS == 0
    in_specs = [pl.BlockSpec(memory_space=pl.ANY),
                _const_spec((1, d)),
                pl.BlockSpec((d, tf), lambda i, j: (0, j)),
                pl.BlockSpec((d, tf), lambda i, j: (0, j)),
                pl.BlockSpec((tf, d), lambda i, j: (j, 0))]
    args = [x, norm_w.reshape(1, d), w_gate.astype(BF16), w_up.astype(BF16),
            w_down.astype(BF16)]
    if final_norm_w is not None:
        in_specs.append(_const_spec((1, d)))
        args.append(final_norm_w.reshape(1, d))
    return pl.pallas_call(
        _ffn_kernel,
        grid=(nt, nj),
        in_specs=in_specs,
        out_specs=pl.BlockSpec(memory_space=pl.ANY),
        out_shape=jax.ShapeDtypeStruct((t, d), F32),
        scratch_shapes=[pltpu.VMEM((2, tm, d), F32),
                        pltpu.VMEM((2, tm, d), BF16),
                        pltpu.SemaphoreType.DMA((2,)),
                        pltpu.SemaphoreType.DMA((2,))],
        compiler_params=_params("arbitrary", "arbitrary"),
        name="ffn",
    )(*args)


def _gelu(z):
    return 0.5 * z * (1.0 + lax.erf(z * (2.0 ** -0.5)))


def _sgu_layer_kernel(x_ref, nw_ref, win_ref, lnw_ref, lnb_ref, ws_ref, bs_ref, wout_ref,
                      o_ref, *, sub):
    tm = x_ref.shape[0]
    dim = lnw_ref.shape[1]
    heads = ws_ref.shape[0]
    hd = dim // heads
    c = SGU_CHUNK

    row = lax.broadcasted_iota(jnp.int32, (c, c), 0)
    col = lax.broadcasted_iota(jnp.int32, (c, c), 1)
    causal = col <= row
    wms = [jnp.where(causal, ws_ref[h], 0.0).astype(BF16) for h in range(heads)]

    for r0 in range(0, tm, sub):
        x = x_ref[r0:r0 + sub, :]
        h = _rms_norm(x, nw_ref[...]).astype(BF16)
        v = _gelu(_dot(h, win_ref[:, dim:]))
        mu = jnp.mean(v, axis=-1, keepdims=True)
        vc = v - mu
        var = jnp.mean(vc * vc, axis=-1, keepdims=True)
        vn = (vc * lax.rsqrt(var + EPS) * lnw_ref[...] + lnb_ref[...]).astype(BF16)
        parts = []
        for hh in range(heads):
            cols = slice(hh * hd, (hh + 1) * hd)
            u = _gelu(_dot(h, win_ref[:, cols]))
            bias = bs_ref[:, hh:hh + 1]
            gate = jnp.concatenate(
                [_dot(wms[hh], vn[n * c:(n + 1) * c, cols]) + bias for n in range(sub // c)],
                axis=0)
            parts.append((u * gate).astype(BF16))
        a = jnp.concatenate(parts, axis=-1)
        o_ref[r0:r0 + sub, :] = x + _dot(a, wout_ref[...])


def _sgu_layer(x, norm_w, w_in, ln_w, ln_b, w_spatial, b_spatial, w_out, *, tm=512, sub=256):
    t, d = x.shape
    dim = w_out.shape[0]
    heads, c, _ = w_spatial.shape
    tm = _tile(t, tm)
    sub = _tile(tm, sub)
    return pl.pallas_call(
        functools.partial(_sgu_layer_kernel, sub=sub),
        grid=(t // tm,),
        in_specs=[pl.BlockSpec((tm, d), lambda i: (i, 0)),
                  _resident_spec((1, d)),
                  _resident_spec((d, 2 * dim)),
                  _resident_spec((1, dim)),
                  _resident_spec((1, dim)),
                  _resident_spec((heads, c, c)),
                  _resident_spec((c, heads)),
                  _resident_spec((dim, d))],
        out_specs=pl.BlockSpec((tm, d), lambda i: (i, 0)),
        out_shape=jax.ShapeDtypeStruct((t, d), F32),
        compiler_params=_params("parallel"),
        name="sgu_layer",
    )(x, norm_w.reshape(1, d), w_in.astype(BF16), ln_w.reshape(1, dim), ln_b.reshape(1, dim),
      w_spatial, b_spatial.T, w_out.astype(BF16))


def _gla_layer_kernel(x_ref, nw_ref, w_ref, wlo_ref, wup_ref, bgk_ref, onw_ref, wout_ref,
                      o_ref, st_ref, *, tiles_per_seq):
    tm = x_ref.shape[0]
    dkt = wup_ref.shape[1]
    dvt = onw_ref.shape[1]
    hk, hv = dkt // GLA_HEADS, dvt // GLA_HEADS
    c = GLA_CHUNK

    @pl.when(pl.program_id(0) % tiles_per_seq == 0)
    def _():
        st_ref[...] = jnp.zeros(st_ref.shape, F32)

    row_k = lax.broadcasted_iota(jnp.int32, (c, hk), 0)
    row = lax.broadcasted_iota(jnp.int32, (c, c), 0)
    col = lax.broadcasted_iota(jnp.int32, (c, c), 1)
    causal = col <= row
    nt = (((1,), (1,)), ((), ()))
    tn = (((0,), (0,)), ((), ()))

    x = x_ref[...]
    hn = _rms_norm(x, nw_ref[...]).astype(BF16)
    low = _dot(hn, wlo_ref[...]).astype(BF16)

    def proj_piece(h, piece, acc):
        kc = slice(h * hk, (h + 1) * hk)
        vc = slice(h * hv, (h + 1) * hv)
        if piece == 0:
            acc["q"] = _dot(hn, w_ref[:, h * hk:(h + 1) * hk]) * (hk ** -0.5)
            acc["g"] = jax.nn.log_sigmoid(
                _dot(low, wup_ref[:, kc]) + bgk_ref[:, kc]) / GLA_GATE_TEMP
        elif piece == 1:
            acc["k"] = _dot(hn, w_ref[:, dkt + h * hk:dkt + (h + 1) * hk])
        elif piece == 2:
            acc["v"] = _dot(hn, w_ref[:, 2 * dkt + h * hv:2 * dkt + (h + 1) * hv]).astype(BF16)
        else:
            r = _dot(hn, w_ref[:, 2 * dkt + dvt + h * hv:2 * dkt + dvt + (h + 1) * hv])
            acc["gate"] = r * jax.nn.sigmoid(r) * onw_ref[:, vc]

    n_chunks = tm // c
    cur = {}
    for piece in range(4):
        proj_piece(0, piece, cur)
    out = x
    for h in range(GLA_HEADS):
        q_all, k_all, v_all, g_all, gate_all = (cur[n] for n in ("q", "k", "v", "g", "gate"))
        nxt = {}
        outs = []
        for n in range(n_chunks):
            if h + 1 < GLA_HEADS:
                for piece in range(n * 4 // n_chunks, (n + 1) * 4 // n_chunks):
                    proj_piece(h + 1, piece, nxt)
            rows = slice(n * c, (n + 1) * c)
            b = g_all[rows]
            shift = 1
            while shift < c:
                b = b + jnp.where(row_k >= shift, pltpu.roll(b, shift, 0), 0.0)
                shift *= 2
            b_last = b[c - 1:c, :]
            k = k_all[rows]
            v = v_all[rows]
            q_in = (q_all[rows] * jnp.exp(b)).astype(BF16)
            k_in = (k * jnp.exp(-b)).astype(BF16)
            k_dec = (k * jnp.exp(b_last - b)).astype(BF16)
            scores = lax.dot_general(q_in, k_in, nt, preferred_element_type=F32)
            scores = jnp.where(causal, scores, 0.0).astype(BF16)
            st = st_ref[h]
            o = _dot(scores, v) + lax.dot_general(q_in, st.astype(BF16), nt,
                                                  preferred_element_type=F32)
            st_ref[h] = jnp.exp(b_last) * st + lax.dot_general(
                v, k_dec, tn, preferred_element_type=F32)
            o = o * lax.rsqrt(jnp.mean(o * o, axis=-1, keepdims=True) + EPS)
            outs.append((o * gate_all[rows]).astype(BF16))
        out = out + _dot(jnp.concatenate(outs, axis=0), wout_ref[h * hv:(h + 1) * hv, :])
        cur = nxt
    o_ref[...] = out


def _gla_layer(x, seq, norm_w, w_in, w_gk_up, b_gk, out_norm_w, w_out, *, tm=256):
    t, d = x.shape
    rank, dk = w_gk_up.shape
    dv = out_norm_w.shape[0]
    tm = _tile(seq, tm)
    hk, hv = dk // GLA_HEADS, dv // GLA_HEADS
    n_main = 2 * dk + 2 * dv
    rank_pad = -(-rank // LANES) * LANES
    w_main = w_in[:, :n_main].astype(BF16)
    w_low = jnp.pad(w_in[:, n_main:], ((0, 0), (0, rank_pad - rank))).astype(BF16)
    w_up = jnp.pad(w_gk_up, ((0, rank_pad - rank), (0, 0))).astype(BF16)
    return pl.pallas_call(
        functools.partial(_gla_layer_kernel, tiles_per_seq=seq // tm),
        grid=(t // tm,),
        in_specs=[pl.BlockSpec((tm, d), lambda i: (i, 0)),
                  _resident_spec((1, d)),
                  _resident_spec((d, n_main)),
                  _resident_spec((d, rank_pad)),
                  _resident_spec((rank_pad, dk)),
                  _resident_spec((1, dk)),
                  _resident_spec((1, dv)),
                  _resident_spec((dv, d))],
        out_specs=pl.BlockSpec((tm, d), lambda i: (i, 0)),
        out_shape=jax.ShapeDtypeStruct((t, d), F32),
        scratch_shapes=[pltpu.VMEM((GLA_HEADS, hv, hk), F32)],
        compiler_params=_params("arbitrary"),
        name="gla_layer",
    )(x, norm_w.reshape(1, d), w_main, w_low, w_up, b_gk.reshape(1, dk),
      out_norm_w.reshape(1, dv), w_out.astype(BF16))


def kernel(x, l0_norm1, l0_pool_w_in, l0_pool_w_group, l0_pool_scale, l0_pool_w_out, l0_norm2, l0_ffn_w_gate, l0_ffn_w_up, l0_ffn_w_down, l1_norm1, l1_sgu_w_in, l1_sgu_v_norm_w, l1_sgu_v_norm_b, l1_sgu_w_spatial, l1_sgu_b_spatial, l1_sgu_w_out, l1_norm2, l1_ffn_w_gate, l1_ffn_w_up, l1_ffn_w_down, l2_norm1, l2_gla_w_in, l2_gla_w_gk_up, l2_gla_b_gk, l2_gla_out_norm_w, l2_gla_w_out, l2_norm2, l2_ffn_w_gate, l2_ffn_w_up, l2_ffn_w_down, l3_norm1, l3_pool_w_in, l3_pool_w_group, l3_pool_scale, l3_pool_w_out, l3_norm2, l3_ffn_w_gate, l3_ffn_w_up, l3_ffn_w_down, final_norm_w):
    batch, seq, d = x.shape
    h = x.reshape(batch * seq, d)
    h = _pool_layer(h, seq, l0_norm1, l0_pool_w_in, l0_pool_w_group, l0_pool_scale, l0_pool_w_out)
    h = _ffn(h, l0_norm2, l0_ffn_w_gate, l0_ffn_w_up, l0_ffn_w_down)
    h = _sgu_layer(h, l1_norm1, l1_sgu_w_in, l1_sgu_v_norm_w, l1_sgu_v_norm_b,
                   l1_sgu_w_spatial, l1_sgu_b_spatial, l1_sgu_w_out)
    h = _ffn(h, l1_norm2, l1_ffn_w_gate, l1_ffn_w_up, l1_ffn_w_down)
    h = _gla_layer(h, seq, l2_norm1, l2_gla_w_in, l2_gla_w_gk_up, l2_gla_b_gk,
                   l2_gla_out_norm_w, l2_gla_w_out)
    h = _ffn(h, l2_norm2, l2_ffn_w_gate, l2_ffn_w_up, l2_ffn_w_down)
    h = _pool_layer(h, seq, l3_norm1, l3_pool_w_in, l3_pool_w_group, l3_pool_scale, l3_pool_w_out)
    h = _ffn(h, l3_norm2, l3_ffn_w_gate, l3_ffn_w_up, l3_ffn_w_down, final_norm_w)
    return h.reshape(batch, seq, d)
```

```python
import functools

import jax
import jax.numpy as jnp
from jax import lax
from jax.experimental import pallas as pl
from jax.experimental.pallas import tpu as pltpu

F32 = jnp.float32
BF16 = jnp.bfloat16

EPS = 1e-6
POOL_WINDOWS = (2, 4, 8, 16)
POOL_HALO = 24
SGU_CHUNK = 128
GLA_HEADS = 4
GLA_CHUNK = 64
GLA_GATE_TEMP = 16.0
LANES = 128
VMEM_LIMIT = 58 * 1024 * 1024


def _params(*semantics):
    return pltpu.CompilerParams(dimension_semantics=semantics,
                                vmem_limit_bytes=VMEM_LIMIT)


def _tile(n, want):
    t = min(n, want)
    assert n % t == 0, (n, want)
    return t


def _rms_norm(x, w):
    ms = jnp.mean(x * x, axis=-1, keepdims=True)
    return x * lax.rsqrt(ms + EPS) * w


def _dot(a, b):
    return jnp.dot(a, b, preferred_element_type=F32)


def _const_spec(shape):
    return pl.BlockSpec(shape, lambda *_: (0,) * len(shape))


def _resident_spec(shape):
    return pl.BlockSpec(shape, lambda *_: (0,) * len(shape),
                        pipeline_mode=pl.Buffered(1))


def _pool_layer_kernel(x_ref, nw_ref, win_ref, wg_ref, sc_ref, wout_ref, o_ref, zbuf,
                       *, tiles_per_seq, sub):
    tm = x_ref.shape[0]
    cg = wg_ref.shape[1]
    pos = pl.program_id(0) % tiles_per_seq
    halo = POOL_HALO

    @pl.when(pos == 0)
    def _():
        zbuf[0:halo, :] = jnp.zeros((halo, zbuf.shape[1]), F32)

    @pl.when(pos != 0)
    def _():
        zbuf[0:halo, :] = zbuf[tm:tm + halo, :]

    ext = sub + 16
    for r0 in range(0, tm, sub):
        x = x_ref[r0:r0 + sub, :]
        h = _rms_norm(x, nw_ref[...]).astype(BF16)
        t_idx = pos * tm + r0 + lax.broadcasted_iota(jnp.int32, (sub, cg), 0)
        ys = []
        for g, w in enumerate(POOL_WINDOWS):
            cols = slice(g * cg, (g + 1) * cg)
            z = _dot(h, win_ref[:, cols])
            zbuf[halo + r0:halo + r0 + sub, cols] = z
            s = zbuf[8 + r0:8 + r0 + ext, cols]
            shift = 1
            while shift < w:
                s = s + pltpu.roll(s, shift, 0)
                shift *= 2
            count = jnp.minimum(t_idx + 1, w).astype(F32)
            p = s[16:, :] / count - z
            y = _dot(p.astype(BF16), wg_ref[g]) * sc_ref[:, cols]
            ys.append(y.astype(BF16))
        y = jnp.concatenate(ys, axis=-1)
        o_ref[r0:r0 + sub, :] = x + _dot(y, wout_ref[...])


def _pool_layer(x, seq, norm_w, w_in, w_group, scale, w_out, *, tm=512, sub=256):
    t, d = x.shape
    p = w_in.shape[1]
    g, cg, _ = w_group.shape
    tm = _tile(seq, tm)
    sub = _tile(tm, sub)
    return pl.pallas_call(
        functools.partial(_pool_layer_kernel, tiles_per_seq=seq // tm, sub=sub),
        grid=(t // tm,),
        in_specs=[pl.BlockSpec((tm, d), lambda i: (i, 0)),
                  _resident_spec((1, d)),
                  _resident_spec((d, p)),
                  _resident_spec((g, cg, cg)),
                  _resident_spec((1, p)),
                  _resident_spec((p, d))],
        out_specs=pl.BlockSpec((tm, d), lambda i: (i, 0)),
        out_shape=jax.ShapeDtypeStruct((t, d), F32),
        scratch_shapes=[pltpu.VMEM((tm + POOL_HALO, p), F32)],
        compiler_params=_params("arbitrary"),
        name="pool_layer",
    )(x, norm_w.reshape(1, d), w_in.astype(BF16), w_group.astype(BF16),
      scale.reshape(1, p), w_out.astype(BF16))


FFN_NORM_STEPS = 4


def _ffn_kernel(x_hbm, nw_ref, wg_ref, wu_ref, wd_ref, *rest):
    fnw_ref = rest[0] if len(rest) == 6 else None
    o_hbm, acc, hbuf, in_sem, out_sem = rest[-5:]
    i, j = pl.program_id(0), pl.program_id(1)
    nt, nj = pl.num_programs(0), pl.num_programs(1)
    slot = i % 2
    other = 1 - slot
    tm = acc.shape[1]
    rq = tm // FFN_NORM_STEPS

    def x_copy(tile, s):
        return pltpu.make_async_copy(x_hbm.at[pl.ds(tile * tm, tm), :], acc.at[s], in_sem.at[s])

    def o_copy(tile, s):
        return pltpu.make_async_copy(acc.at[s], o_hbm.at[pl.ds(tile * tm, tm), :], out_sem.at[s])

    @pl.when((i == 0) & (j == 0))
    def _():
        first = x_copy(0, 0)
        first.start()
        first.wait()
        hbuf[0] = _rms_norm(acc[0], nw_ref[...]).astype(BF16)

    @pl.when((j == 1) & (i >= 1))
    def _():
        o_copy(i - 1, other).wait()

    @pl.when((j == 1) & (i + 1 < nt))
    def _():
        x_copy(i + 1, other).start()

    q = j - (nj - FFN_NORM_STEPS)
    norm_next = (q >= 0) & (i + 1 < nt)

    @pl.when((q == 0) & (i + 1 < nt))
    def _():
        x_copy(i + 1, other).wait()

    def step(with_norm):
        h = hbuf[slot]
        g = _dot(h, wg_ref[...].astype(BF16))
        u = _dot(h, wu_ref[...].astype(BF16))
        a = (g * jax.nn.sigmoid(g) * u).astype(BF16)
        acc[slot] += _dot(a, wd_ref[...].astype(BF16))
        if with_norm:
            rows = pl.ds(pl.multiple_of(q * rq, rq), rq)
            hbuf[other, rows, :] = _rms_norm(acc[other, rows, :], nw_ref[...]).astype(BF16)

    pl.when(norm_next)(functools.partial(step, True))
    pl.when(jnp.logical_not(norm_next))(functools.partial(step, False))

    @pl.when(j == nj - 1)
    def _():
        if fnw_ref is not None:
            acc[slot] = _rms_norm(acc[slot], fnw_ref[...])
        o_copy(i, slot).start()

    @pl.when((j == nj - 1) & (i == nt - 1))
    def _():
        o_copy(i, slot).wait()


def _ffn(x, norm_w, w_gate, w_up, w_down, final_norm_w=None, *, tm=1024, tf=512):
    t, d = x.shape
    f = w_gate.shape[1]
    tm, tf = _tile(t, tm), _tile(f, tf)
    nt, nj = t // tm, f // tf
    assert nt >= 2 and nj >= FFN_NORM_STEPS + 2 and tm % FFN_NORM_STEPS == 0
    in_specs = [pl.BlockSpec(memory_space=pl.ANY),
                _const_spec((1, d)),
                pl.BlockSpec((d, tf), lambda i, j: (0, j)),
                pl.BlockSpec((d, tf), lambda i, j: (0, j)),
                pl.BlockSpec((tf, d), lambda i, j: (j, 0))]
    args = [x, norm_w.reshape(1, d), w_gate, w_up, w_down]
    if final_norm_w is not None:
        in_specs.append(_const_spec((1, d)))
        args.append(final_norm_w.reshape(1, d))
    return pl.pallas_call(
        _ffn_kernel,
        grid=(nt, nj),
        in_specs=in_specs,
        out_specs=pl.BlockSpec(memory_space=pl.ANY),
        out_shape=jax.ShapeDtypeStruct((t, d), F32),
        scratch_shapes=[pltpu.VMEM((2, tm, d), F32),
                        pltpu.VMEM((2, tm, d), BF16),
                        pltpu.SemaphoreType.DMA((2,)),
                        pltpu.SemaphoreType.DMA((2,))],
        compiler_params=_params("arbitrary", "arbitrary"),
        name="ffn",
    )(*args)


def _gelu(z):
    return 0.5 * z * (1.0 + lax.erf(z * (2.0 ** -0.5)))


def _sgu_layer_kernel(x_ref, nw_ref, win_ref, lnw_ref, lnb_ref, ws_ref, bs_ref, wout_ref,
                      o_ref, *, sub):
    tm = x_ref.shape[0]
    dim = lnw_ref.shape[1]
    heads = ws_ref.shape[0]
    hd = dim // heads
    c = SGU_CHUNK

    row = lax.broadcasted_iota(jnp.int32, (c, c), 0)
    col = lax.broadcasted_iota(jnp.int32, (c, c), 1)
    causal = col <= row
    wms = [jnp.where(causal, ws_ref[h], 0.0).astype(BF16) for h in range(heads)]

    for r0 in range(0, tm, sub):
        x = x_ref[r0:r0 + sub, :]
        h = _rms_norm(x, nw_ref[...]).astype(BF16)
        v = _gelu(_dot(h, win_ref[:, dim:]))
        mu = jnp.mean(v, axis=-1, keepdims=True)
        vc = v - mu
        var = jnp.mean(vc * vc, axis=-1, keepdims=True)
        vn = (vc * lax.rsqrt(var + EPS) * lnw_ref[...] + lnb_ref[...]).astype(BF16)
        parts = []
        for hh in range(heads):
            cols = slice(hh * hd, (hh + 1) * hd)
            u = _gelu(_dot(h, win_ref[:, cols]))
            bias = bs_ref[:, hh:hh + 1]
            gate = jnp.concatenate(
                [_dot(wms[hh], vn[n * c:(n + 1) * c, cols]) + bias for n in range(sub // c)],
                axis=0)
            parts.append((u * gate).astype(BF16))
        a = jnp.concatenate(parts, axis=-1)
        o_ref[r0:r0 + sub, :] = x + _dot(a, wout_ref[...])


def _sgu_layer(x, norm_w, w_in, ln_w, ln_b, w_spatial, b_spatial, w_out, *, tm=512, sub=256):
    t, d = x.shape
    dim = w_out.shape[0]
    heads, c, _ = w_spatial.shape
    tm = _tile(t, tm)
    sub = _tile(tm, sub)
    return pl.pallas_call(
        functools.partial(_sgu_layer_kernel, sub=sub),
        grid=(t // tm,),
        in_specs=[pl.BlockSpec((tm, d), lambda i: (i, 0)),
                  _resident_spec((1, d)),
                  _resident_spec((d, 2 * dim)),
                  _resident_spec((1, dim)),
                  _resident_spec((1, dim)),
                  _resident_spec((heads, c, c)),
                  _resident_spec((c, heads)),
                  _resident_spec((dim, d))],
        out_specs=pl.BlockSpec((tm, d), lambda i: (i, 0)),
        out_shape=jax.ShapeDtypeStruct((t, d), F32),
        compiler_params=_params("parallel"),
        name="sgu_layer",
    )(x, norm_w.reshape(1, d), w_in.astype(BF16), ln_w.reshape(1, dim), ln_b.reshape(1, dim),
      w_spatial, b_spatial.T, w_out.astype(BF16))


def _gla_layer_kernel(x_ref, nw_ref, w_ref, wlo_ref, wup_ref, bgk_ref, onw_ref, wout_ref,
                      o_ref, st_ref, *, tiles_per_seq):
    tm = x_ref.shape[0]
    dkt = wup_ref.shape[1]
    dvt = onw_ref.shape[1]
    hk, hv = dkt // GLA_HEADS, dvt // GLA_HEADS
    c = GLA_CHUNK
    heads = range(GLA_HEADS)

    @pl.when(pl.program_id(0) % tiles_per_seq == 0)
    def _():
        st_ref[...] = jnp.zeros(st_ref.shape, F32)

    row_k = lax.broadcasted_iota(jnp.int32, (2 * c, hk), 0)
    in_a = row_k < c
    row_in_chunk = jnp.where(in_a, row_k, row_k - c)
    row = lax.broadcasted_iota(jnp.int32, (c, c), 0)
    col = lax.broadcasted_iota(jnp.int32, (c, c), 1)
    causal = col <= row
    nt = (((1,), (1,)), ((), ()))
    tn = (((0,), (0,)), ((), ()))

    x = x_ref[...]
    hn = _rms_norm(x, nw_ref[...]).astype(BF16)
    low = _dot(hn, wlo_ref[...]).astype(BF16)
    q_all = _dot(hn, w_ref[:, 0:dkt]) * (hk ** -0.5)
    k_all = _dot(hn, w_ref[:, dkt:2 * dkt])
    g_all = jax.nn.log_sigmoid(_dot(low, wup_ref[...]) + bgk_ref[...]) / GLA_GATE_TEMP
    v_all = _dot(hn, w_ref[:, 2 * dkt:2 * dkt + dvt]).astype(BF16)
    r_all = _dot(hn, w_ref[:, 2 * dkt + dvt:])
    gate_all = r_all * jax.nn.sigmoid(r_all) * onw_ref[...]

    row_parts = []
    for s in range(tm // (2 * c)):
        rows = slice(s * 2 * c, (s + 1) * 2 * c)
        rows_a = slice(s * 2 * c, s * 2 * c + c)
        q_in, k_in, k_dec_a, q2, k2, dc_pair = [], [], [], [], [], []
        for h in heads:
            kc = slice(h * hk, (h + 1) * hk)
            b = g_all[rows, kc]
            shift = 1
            while shift < c:
                b = b + jnp.where(row_in_chunk >= shift, pltpu.roll(b, shift, 0), 0.0)
                shift *= 2
            b_last_a, b_last_b = b[c - 1:c, :], b[2 * c - 1:2 * c, :]
            dc_a, dc_b = jnp.exp(b_last_a), jnp.exp(b_last_b)
            k = k_all[rows, kc]
            qi = q_all[rows, kc] * jnp.exp(b)
            kd = k * jnp.exp(jnp.where(in_a, b_last_a, b_last_b) - b)
            q_in.append(qi.astype(BF16))
            k_in.append((k * jnp.exp(-b)).astype(BF16))
            k_dec_a.append(kd[:c].astype(BF16))
            q2.append(jnp.where(in_a, qi, qi * dc_a).astype(BF16))
            k2.append(jnp.where(in_a, kd * dc_b, kd).astype(BF16))
            dc_pair.append(dc_a * dc_b)
        s_aa = [lax.dot_general(q_in[h][:c], k_in[h][:c], nt, preferred_element_type=F32)
                for h in heads]
        s_bb = [lax.dot_general(q_in[h][c:], k_in[h][c:], nt, preferred_element_type=F32)
                for h in heads]
        s_ba = [lax.dot_general(q_in[h][c:], k_dec_a[h], nt, preferred_element_type=F32)
                for h in heads]
        p_a = [jnp.where(causal, s_aa[h], 0.0).astype(BF16) for h in heads]
        p_b = [jnp.concatenate([s_ba[h], jnp.where(causal, s_bb[h], 0.0)], axis=1).astype(BF16)
               for h in heads]
        o = []
        for h in heads:
            vc = slice(h * hv, (h + 1) * hv)
            intra = jnp.concatenate([_dot(p_a[h], v_all[rows_a, vc]),
                                     _dot(p_b[h], v_all[rows, vc])], axis=0)
            inter = lax.dot_general(q2[h], st_ref[h].astype(BF16), nt,
                                    preferred_element_type=F32)
            o.append(intra + inter)
        for h in heads:
            vc = slice(h * hv, (h + 1) * hv)
            st_ref[h] = dc_pair[h] * st_ref[h] + lax.dot_general(
                v_all[rows, vc], k2[h], tn, preferred_element_type=F32)
        parts = []
        for h in heads:
            vc = slice(h * hv, (h + 1) * hv)
            on = o[h] * lax.rsqrt(jnp.mean(o[h] * o[h], axis=-1, keepdims=True) + EPS)
            parts.append((on * gate_all[rows, vc]).astype(BF16))
        row_parts.append(jnp.concatenate(parts, axis=-1))
    a = jnp.concatenate(row_parts, axis=0)
    o_ref[...] = x + _dot(a, wout_ref[...])


def _gla_layer(x, seq, norm_w, w_in, w_gk_up, b_gk, out_norm_w, w_out, *, tm=256):
    t, d = x.shape
    rank, dk = w_gk_up.shape
    dv = out_norm_w.shape[0]
    tm = _tile(seq, tm)
    hk, hv = dk // GLA_HEADS, dv // GLA_HEADS
    n_main = 2 * dk + 2 * dv
    assert n_main % LANES == 0
    rank_pad = -(-rank // LANES) * LANES
    w_main = w_in.astype(BF16)
    w_low = jnp.pad(w_in[:, n_main:], ((0, 0), (0, rank_pad - rank))).astype(BF16)
    w_up = jnp.pad(w_gk_up, ((0, rank_pad - rank), (0, 0))).astype(BF16)
    return pl.pallas_call(
        functools.partial(_gla_layer_kernel, tiles_per_seq=seq // tm),
        grid=(t // tm,),
        in_specs=[pl.BlockSpec((tm, d), lambda i: (i, 0)),
                  _resident_spec((1, d)),
                  _resident_spec((d, n_main)),
                  _resident_spec((d, rank_pad)),
                  _resident_spec((rank_pad, dk)),
                  _resident_spec((1, dk)),
                  _resident_spec((1, dv)),
                  _resident_spec((dv, d))],
        out_specs=pl.BlockSpec((tm, d), lambda i: (i, 0)),
        out_shape=jax.ShapeDtypeStruct((t, d), F32),
        scratch_shapes=[pltpu.VMEM((GLA_HEADS, hv, hk), F32)],
        compiler_params=_params("arbitrary"),
        name="gla_layer",
    )(x, norm_w.reshape(1, d), w_main, w_low, w_up, b_gk.reshape(1, dk),
      out_norm_w.reshape(1, dv), w_out.astype(BF16))


def kernel(x, l0_norm1, l0_pool_w_in, l0_pool_w_group, l0_pool_scale, l0_pool_w_out, l0_norm2, l0_ffn_w_gate, l0_ffn_w_up, l0_ffn_w_down, l1_norm1, l1_sgu_w_in, l1_sgu_v_norm_w, l1_sgu_v_norm_b, l1_sgu_w_spatial, l1_sgu_b_spatial, l1_sgu_w_out, l1_norm2, l1_ffn_w_gate, l1_ffn_w_up, l1_ffn_w_down, l2_norm1, l2_gla_w_in, l2_gla_w_gk_up, l2_gla_b_gk, l2_gla_out_norm_w, l2_gla_w_out, l2_norm2, l2_ffn_w_gate, l2_ffn_w_up, l2_ffn_w_down, l3_norm1, l3_pool_w_in, l3_pool_w_group, l3_pool_scale, l3_pool_w_out, l3_norm2, l3_ffn_w_gate, l3_ffn_w_up, l3_ffn_w_down, final_norm_w):
    batch, seq, d = x.shape
    h = x.reshape(batch * seq, d)
    h = _pool_layer(h, seq, l0_norm1, l0_pool_w_in, l0_pool_w_group, l0_pool_scale, l0_pool_w_out)
    h = _ffn(h, l0_norm2, l0_ffn_w_gate, l0_ffn_w_up, l0_ffn_w_down)
    h = _sgu_layer(h, l1_norm1, l1_sgu_w_in, l1_sgu_v_norm_w, l1_sgu_v_norm_b,
                   l1_sgu_w_spatial, l1_sgu_b_spatial, l1_sgu_w_out)
    h = _ffn(h, l1_norm2, l1_ffn_w_gate, l1_ffn_w_up, l1_ffn_w_down)
    h = _gla_layer(h, seq, l2_norm1, l2_gla_w_in, l2_gla_w_gk_up, l2_gla_b_gk,
                   l2_gla_out_norm_w, l2_gla_w_out)
    h = _ffn(h, l2_norm2, l2_ffn_w_gate, l2_ffn_w_up, l2_ffn_w_down)
    h = _pool_layer(h, seq, l3_norm1, l3_pool_w_in, l3_pool_w_group, l3_pool_scale, l3_pool_w_out)
    h = _ffn(h, l3_norm2, l3_ffn_w_gate, l3_ffn_w_up, l3_ffn_w_down, final_norm_w)
    return h.reshape(batch, seq, d)
```

```python
import functools

import jax
import jax.numpy as jnp
from jax import lax
from jax.experimental import pallas as pl
from jax.experimental.pallas import tpu as pltpu

F32 = jnp.float32
BF16 = jnp.bfloat16

EPS = 1e-6
POOL_WINDOWS = (2, 4, 8, 16)
POOL_HALO = 24
SGU_CHUNK = 128
GLA_HEADS = 4
GLA_CHUNK = 64
GLA_GATE_TEMP = 16.0
LANES = 128
VMEM_LIMIT = 58 * 1024 * 1024


def _params(*semantics):
    return pltpu.CompilerParams(dimension_semantics=semantics,
                                vmem_limit_bytes=VMEM_LIMIT)


def _tile(n, want):
    t = min(n, want)
    assert n % t == 0, (n, want)
    return t


def _rms_norm(x, w):
    ms = jnp.mean(x * x, axis=-1, keepdims=True)
    return x * lax.rsqrt(ms + EPS) * w


def _dot(a, b):
    return jnp.dot(a, b, preferred_element_type=F32)


def _const_spec(shape):
    return pl.BlockSpec(shape, lambda *_: (0,) * len(shape))


def _resident_spec(shape):
    return pl.BlockSpec(shape, lambda *_: (0,) * len(shape),
                        pipeline_mode=pl.Buffered(1))


BF16_SUBLANES = 16


def _cast_plan(steps, weights):
    in_specs, out_specs, out_shapes = [], [], []
    for w in weights:
        rows, cols = w.shape
        rep = 1
        while (rows * rep) % (steps * BF16_SUBLANES):
            rep *= 2
            assert rep <= steps
        blk = rows * rep // steps
        index_map = functools.partial(lambda i, rep: (i // rep, 0), rep=rep)
        in_specs.append(pl.BlockSpec((blk, cols), index_map))
        out_specs.append(pl.BlockSpec((blk, cols), index_map))
        out_shapes.append(jax.ShapeDtypeStruct((rows, cols), BF16))
    return in_specs, out_specs, out_shapes


def _cast_blocks(src_refs, dst_refs):
    for src, dst in zip(src_refs, dst_refs):
        dst[...] = src[...].astype(BF16)


def _pool_layer_kernel(x_ref, nw_ref, win_ref, wg_ref, sc_ref, wout_ref, c0, c1, c2,
                       o_ref, d0, d1, d2, zbuf, *, tiles_per_seq, sub):
    _cast_blocks((c0, c1, c2), (d0, d1, d2))
    tm = x_ref.shape[0]
    cg = wg_ref.shape[1]
    pos = pl.program_id(0) % tiles_per_seq
    halo = POOL_HALO

    @pl.when(pos == 0)
    def _():
        zbuf[0:halo, :] = jnp.zeros((halo, zbuf.shape[1]), F32)

    @pl.when(pos != 0)
    def _():
        zbuf[0:halo, :] = zbuf[tm:tm + halo, :]

    ext = sub + 16
    for r0 in range(0, tm, sub):
        x = x_ref[r0:r0 + sub, :]
        h = _rms_norm(x, nw_ref[...]).astype(BF16)
        t_idx = pos * tm + r0 + lax.broadcasted_iota(jnp.int32, (sub, cg), 0)
        ys = []
        for g, w in enumerate(POOL_WINDOWS):
            cols = slice(g * cg, (g + 1) * cg)
            z = _dot(h, win_ref[:, cols])
            zbuf[halo + r0:halo + r0 + sub, cols] = z
            s = zbuf[8 + r0:8 + r0 + ext, cols]
            shift = 1
            while shift < w:
                s = s + pltpu.roll(s, shift, 0)
                shift *= 2
            count = jnp.minimum(t_idx + 1, w).astype(F32)
            p = s[16:, :] / count - z
            y = _dot(p.astype(BF16), wg_ref[g]) * sc_ref[:, cols]
            ys.append(y.astype(BF16))
        y = jnp.concatenate(ys, axis=-1)
        o_ref[r0:r0 + sub, :] = x + _dot(y, wout_ref[...])


def _pool_layer(x, seq, norm_w, w_in, w_group, scale, w_out, ffn_weights, *, tm=512, sub=256):
    t, d = x.shape
    p = w_in.shape[1]
    g, cg, _ = w_group.shape
    tm = _tile(seq, tm)
    sub = _tile(tm, sub)
    cast_in, cast_out, cast_shapes = _cast_plan(t // tm, ffn_weights)
    out, *ffn_bf16 = pl.pallas_call(
        functools.partial(_pool_layer_kernel, tiles_per_seq=seq // tm, sub=sub),
        grid=(t // tm,),
        in_specs=[pl.BlockSpec((tm, d), lambda i: (i, 0)),
                  _resident_spec((1, d)),
                  _resident_spec((d, p)),
                  _resident_spec((g, cg, cg)),
                  _resident_spec((1, p)),
                  _resident_spec((p, d))] + cast_in,
        out_specs=[pl.BlockSpec((tm, d), lambda i: (i, 0))] + cast_out,
        out_shape=[jax.ShapeDtypeStruct((t, d), F32)] + cast_shapes,
        scratch_shapes=[pltpu.VMEM((tm + POOL_HALO, p), F32)],
        compiler_params=_params("arbitrary"),
        name="pool_layer",
    )(x, norm_w.reshape(1, d), w_in.astype(BF16), w_group.astype(BF16),
      scale.reshape(1, p), w_out.astype(BF16), *ffn_weights)
    return out, ffn_bf16


FFN_NORM_STEPS = 4


def _ffn_kernel(x_hbm, nw_ref, wg_ref, wu_ref, wd_ref, *rest):
    fnw_ref = rest[0] if len(rest) == 8 else None
    o_hbm, acc0, acc1, hbuf0, hbuf1, in_sem, out_sem = rest[-7:]
    acc, hbuf = (acc0, acc1), (hbuf0, hbuf1)
    i, j = pl.program_id(0), pl.program_id(1)
    nt, nj = pl.num_programs(0), pl.num_programs(1)
    tm = acc0.shape[0]
    rq = tm // FFN_NORM_STEPS
    q = j - (nj - FFN_NORM_STEPS)
    has_next = i + 1 < nt

    def x_copy(tile, s):
        return pltpu.make_async_copy(x_hbm.at[pl.ds(tile * tm, tm), :], acc[s], in_sem.at[s])

    def o_copy(tile, s):
        return pltpu.make_async_copy(acc[s], o_hbm.at[pl.ds(tile * tm, tm), :], out_sem.at[s])

    @pl.when((i == 0) & (j == 0))
    def _():
        first = x_copy(0, 0)
        first.start()
        first.wait()
        hbuf[0][...] = _rms_norm(acc[0][...], nw_ref[...]).astype(BF16)

    def step(slot, with_norm):
        other = 1 - slot
        h = hbuf[slot][...]
        tfh = wg_ref.shape[1] // 2
        rp = rq // 4

        def norm_piece(p):
            if with_norm:
                rows = pl.ds(pl.multiple_of(q * rq + p * rp, rp), rp)
                hbuf[other][rows, :] = _rms_norm(acc[other][rows, :], nw_ref[...]).astype(BF16)

        parts = []
        for half in range(2):
            cols = slice(half * tfh, (half + 1) * tfh)
            g = _dot(h, wg_ref[:, cols].astype(BF16))
            norm_piece(2 * half)
            u = _dot(h, wu_ref[:, cols].astype(BF16))
            norm_piece(2 * half + 1)
            parts.append((g * jax.nn.sigmoid(g) * u).astype(BF16))
        a = jnp.concatenate(parts, axis=-1)
        acc[slot][...] += _dot(a, wd_ref[...].astype(BF16))

    for slot in (0, 1):
        other = 1 - slot
        mine = (i % 2) == slot

        @pl.when(mine & (j == 1) & (i >= 1))
        def _():
            o_copy(i - 1, other).wait()

        @pl.when(mine & (j == 1) & has_next)
        def _():
            x_copy(i + 1, other).start()

        @pl.when(mine & (q == 0) & has_next)
        def _():
            x_copy(i + 1, other).wait()

        norm_next = (q >= 0) & has_next
        pl.when(mine & norm_next)(functools.partial(step, slot, True))
        pl.when(mine & jnp.logical_not(norm_next))(functools.partial(step, slot, False))

        @pl.when(mine & (j == nj - 1))
        def _():
            if fnw_ref is not None:
                acc[slot][...] = _rms_norm(acc[slot][...], fnw_ref[...])
            o_copy(i, slot).start()

        @pl.when(mine & (j == nj - 1) & (i == nt - 1))
        def _():
            o_copy(i, slot).wait()


def _ffn(x, norm_w, w_gate, w_up, w_down, final_norm_w=None, *, tm=1024, tf=512):
    t, d = x.shape
    f = w_gate.shape[1]
    tm, tf = _tile(t, tm), _tile(f, tf)
    nt, nj = t // tm, f // tf
    assert nt >= 2 and nj >= FFN_NORM_STEPS + 2 and tm % FFN_NORM_STEPS == 0
    in_specs = [pl.BlockSpec(memory_space=pl.ANY),
                _const_spec((1, d)),
                pl.BlockSpec((d, tf), lambda i, j: (0, j)),
                pl.BlockSpec((d, tf), lambda i, j: (0, j)),
                pl.BlockSpec((tf, d), lambda i, j: (j, 0))]
    args = [x, norm_w.reshape(1, d), w_gate, w_up, w_down]
    if final_norm_w is not None:
        in_specs.append(_const_spec((1, d)))
        args.append(final_norm_w.reshape(1, d))
    return pl.pallas_call(
        _ffn_kernel,
        grid=(nt, nj),
        in_specs=in_specs,
        out_specs=pl.BlockSpec(memory_space=pl.ANY),
        out_shape=jax.ShapeDtypeStruct((t, d), F32),
        scratch_shapes=[pltpu.VMEM((tm, d), F32), pltpu.VMEM((tm, d), F32),
                        pltpu.VMEM((tm, d), BF16), pltpu.VMEM((tm, d), BF16),
                        pltpu.SemaphoreType.DMA((2,)),
                        pltpu.SemaphoreType.DMA((2,))],
        compiler_params=_params("arbitrary", "arbitrary"),
        name="ffn",
    )(*args)


def _gelu(z):
    return 0.5 * z * (1.0 + lax.erf(z * (2.0 ** -0.5)))


def _sgu_layer_kernel(x_ref, nw_ref, win_ref, lnw_ref, lnb_ref, ws_ref, bs_ref, wout_ref,
                      c0, c1, c2, o_ref, d0, d1, d2, *, sub):
    _cast_blocks((c0, c1, c2), (d0, d1, d2))
    tm = x_ref.shape[0]
    dim = lnw_ref.shape[1]
    heads = ws_ref.shape[0]
    hd = dim // heads
    c = SGU_CHUNK

    row = lax.broadcasted_iota(jnp.int32, (c, c), 0)
    col = lax.broadcasted_iota(jnp.int32, (c, c), 1)
    causal = col <= row
    wms = [jnp.where(causal, ws_ref[h], 0.0).astype(BF16) for h in range(heads)]

    for r0 in range(0, tm, sub):
        x = x_ref[r0:r0 + sub, :]
        h = _rms_norm(x, nw_ref[...]).astype(BF16)
        v = _gelu(_dot(h, win_ref[:, dim:]))
        mu = jnp.mean(v, axis=-1, keepdims=True)
        vc = v - mu
        var = jnp.mean(vc * vc, axis=-1, keepdims=True)
        vn = (vc * lax.rsqrt(var + EPS) * lnw_ref[...] + lnb_ref[...]).astype(BF16)
        parts = []
        for hh in range(heads):
            cols = slice(hh * hd, (hh + 1) * hd)
            u = _gelu(_dot(h, win_ref[:, cols]))
            bias = bs_ref[:, hh:hh + 1]
            gate = jnp.concatenate(
                [_dot(wms[hh], vn[n * c:(n + 1) * c, cols]) + bias for n in range(sub // c)],
                axis=0)
            parts.append((u * gate).astype(BF16))
        a = jnp.concatenate(parts, axis=-1)
        o_ref[r0:r0 + sub, :] = x + _dot(a, wout_ref[...])


def _sgu_layer(x, norm_w, w_in, ln_w, ln_b, w_spatial, b_spatial, w_out, ffn_weights,
               *, tm=512, sub=256):
    t, d = x.shape
    dim = w_out.shape[0]
    heads, c, _ = w_spatial.shape
    tm = _tile(t, tm)
    sub = _tile(tm, sub)
    cast_in, cast_out, cast_shapes = _cast_plan(t // tm, ffn_weights)
    out, *ffn_bf16 = pl.pallas_call(
        functools.partial(_sgu_layer_kernel, sub=sub),
        grid=(t // tm,),
        in_specs=[pl.BlockSpec((tm, d), lambda i: (i, 0)),
                  _resident_spec((1, d)),
                  _resident_spec((d, 2 * dim)),
                  _resident_spec((1, dim)),
                  _resident_spec((1, dim)),
                  _resident_spec((heads, c, c)),
                  _resident_spec((c, heads)),
                  _resident_spec((dim, d))] + cast_in,
        out_specs=[pl.BlockSpec((tm, d), lambda i: (i, 0))] + cast_out,
        out_shape=[jax.ShapeDtypeStruct((t, d), F32)] + cast_shapes,
        compiler_params=_params("arbitrary"),
        name="sgu_layer",
    )(x, norm_w.reshape(1, d), w_in.astype(BF16), ln_w.reshape(1, dim), ln_b.reshape(1, dim),
      w_spatial, b_spatial.T, w_out.astype(BF16), *ffn_weights)
    return out, ffn_bf16


def _gla_layer_kernel(x_ref, nw_ref, w_ref, wlo_ref, wup_ref, bgk_ref, onw_ref, wout_ref,
                      o_ref, st_ref, *, tiles_per_seq):
    tm = x_ref.shape[0]
    dkt = wup_ref.shape[1]
    dvt = onw_ref.shape[1]
    hk, hv = dkt // GLA_HEADS, dvt // GLA_HEADS
    c = GLA_CHUNK
    heads = range(GLA_HEADS)

    @pl.when(pl.program_id(0) % tiles_per_seq == 0)
    def _():
        st_ref[...] = jnp.zeros(st_ref.shape, F32)

    row_k = lax.broadcasted_iota(jnp.int32, (2 * c, hk), 0)
    in_a = row_k < c
    row_in_chunk = jnp.where(in_a, row_k, row_k - c)
    row = lax.broadcasted_iota(jnp.int32, (c, c), 0)
    col = lax.broadcasted_iota(jnp.int32, (c, c), 1)
    causal = col <= row
    nt = (((1,), (1,)), ((), ()))
    tn = (((0,), (0,)), ((), ()))

    x = x_ref[...]
    hn = _rms_norm(x, nw_ref[...]).astype(BF16)
    low = _dot(hn, wlo_ref[...]).astype(BF16)
    q_all = _dot(hn, w_ref[:, 0:dkt]) * (hk ** -0.5)
    k_all = _dot(hn, w_ref[:, dkt:2 * dkt])
    g_all = jax.nn.log_sigmoid(_dot(low, wup_ref[...]) + bgk_ref[...]) / GLA_GATE_TEMP
    v_all = _dot(hn, w_ref[:, 2 * dkt:2 * dkt + dvt]).astype(BF16)
    r_all = _dot(hn, w_ref[:, 2 * dkt + dvt:])
    gate_all = r_all * jax.nn.sigmoid(r_all) * onw_ref[...]

    row_parts = []
    for s in range(tm // (2 * c)):
        rows = slice(s * 2 * c, (s + 1) * 2 * c)
        rows_a = slice(s * 2 * c, s * 2 * c + c)
        q_in, k_in, k_dec_a, q2, k2, dc_pair = [], [], [], [], [], []
        for h in heads:
            kc = slice(h * hk, (h + 1) * hk)
            b = g_all[rows, kc]
            shift = 1
            while shift < c:
                b = b + jnp.where(row_in_chunk >= shift, pltpu.roll(b, shift, 0), 0.0)
                shift *= 2
            b_last_a, b_last_b = b[c - 1:c, :], b[2 * c - 1:2 * c, :]
            dc_a, dc_b = jnp.exp(b_last_a), jnp.exp(b_last_b)
            k = k_all[rows, kc]
            qi = q_all[rows, kc] * jnp.exp(b)
            kd = k * jnp.exp(jnp.where(in_a, b_last_a, b_last_b) - b)
            q_in.append(qi.astype(BF16))
            k_in.append((k * jnp.exp(-b)).astype(BF16))
            k_dec_a.append(kd[:c].astype(BF16))
            q2.append(jnp.where(in_a, qi, qi * dc_a).astype(BF16))
            k2.append(jnp.where(in_a, kd * dc_b, kd).astype(BF16))
            dc_pair.append(dc_a * dc_b)
        s_aa = [lax.dot_general(q_in[h][:c], k_in[h][:c], nt, preferred_element_type=F32)
                for h in heads]
        s_bb = [lax.dot_general(q_in[h][c:], k_in[h][c:], nt, preferred_element_type=F32)
                for h in heads]
        s_ba = [lax.dot_general(q_in[h][c:], k_dec_a[h], nt, preferred_element_type=F32)
                for h in heads]
        p_a = [jnp.where(causal, s_aa[h], 0.0).astype(BF16) for h in heads]
        p_b = [jnp.concatenate([s_ba[h], jnp.where(causal, s_bb[h], 0.0)], axis=1).astype(BF16)
               for h in heads]
        o = []
        for h in heads:
            vc = slice(h * hv, (h + 1) * hv)
            intra = jnp.concatenate([_dot(p_a[h], v_all[rows_a, vc]),
                                     _dot(p_b[h], v_all[rows, vc])], axis=0)
            inter = lax.dot_general(q2[h], st_ref[h].astype(BF16), nt,
                                    preferred_element_type=F32)
            o.append(intra + inter)
        for h in heads:
            vc = slice(h * hv, (h + 1) * hv)
            st_ref[h] = dc_pair[h] * st_ref[h] + lax.dot_general(
                v_all[rows, vc], k2[h], tn, preferred_element_type=F32)
        parts = []
        for h in heads:
            vc = slice(h * hv, (h + 1) * hv)
            on = o[h] * lax.rsqrt(jnp.mean(o[h] * o[h], axis=-1, keepdims=True) + EPS)
            parts.append((on * gate_all[rows, vc]).astype(BF16))
        row_parts.append(jnp.concatenate(parts, axis=-1))
    a = jnp.concatenate(row_parts, axis=0)
    o_ref[...] = x + _dot(a, wout_ref[...])


def _gla_layer(x, seq, norm_w, w_in, w_gk_up, b_gk, out_norm_w, w_out, *, tm=256):
    t, d = x.shape
    rank, dk = w_gk_up.shape
    dv = out_norm_w.shape[0]
    tm = _tile(seq, tm)
    hk, hv = dk // GLA_HEADS, dv // GLA_HEADS
    n_main = 2 * dk + 2 * dv
    assert n_main % LANES == 0
    rank_pad = -(-rank // LANES) * LANES
    w_main = w_in.astype(BF16)
    w_low = jnp.pad(w_in[:, n_main:], ((0, 0), (0, rank_pad - rank))).astype(BF16)
    w_up = jnp.pad(w_gk_up, ((0, rank_pad - rank), (0, 0))).astype(BF16)
    return pl.pallas_call(
        functools.partial(_gla_layer_kernel, tiles_per_seq=seq // tm),
        grid=(t // tm,),
        in_specs=[pl.BlockSpec((tm, d), lambda i: (i, 0)),
                  _resident_spec((1, d)),
                  _resident_spec((d, n_main)),
                  _resident_spec((d, rank_pad)),
                  _resident_spec((rank_pad, dk)),
                  _resident_spec((1, dk)),
                  _resident_spec((1, dv)),
                  _resident_spec((dv, d))],
        out_specs=pl.BlockSpec((tm, d), lambda i: (i, 0)),
        out_shape=jax.ShapeDtypeStruct((t, d), F32),
        scratch_shapes=[pltpu.VMEM((GLA_HEADS, hv, hk), F32)],
        compiler_params=_params("arbitrary"),
        name="gla_layer",
    )(x, norm_w.reshape(1, d), w_main, w_low, w_up, b_gk.reshape(1, dk),
      out_norm_w.reshape(1, dv), w_out.astype(BF16))


def kernel(x, l0_norm1, l0_pool_w_in, l0_pool_w_group, l0_pool_scale, l0_pool_w_out, l0_norm2, l0_ffn_w_gate, l0_ffn_w_up, l0_ffn_w_down, l1_norm1, l1_sgu_w_in, l1_sgu_v_norm_w, l1_sgu_v_norm_b, l1_sgu_w_spatial, l1_sgu_b_spatial, l1_sgu_w_out, l1_norm2, l1_ffn_w_gate, l1_ffn_w_up, l1_ffn_w_down, l2_norm1, l2_gla_w_in, l2_gla_w_gk_up, l2_gla_b_gk, l2_gla_out_norm_w, l2_gla_w_out, l2_norm2, l2_ffn_w_gate, l2_ffn_w_up, l2_ffn_w_down, l3_norm1, l3_pool_w_in, l3_pool_w_group, l3_pool_scale, l3_pool_w_out, l3_norm2, l3_ffn_w_gate, l3_ffn_w_up, l3_ffn_w_down, final_norm_w):
    batch, seq, d = x.shape
    h = x.reshape(batch * seq, d)
    h, ffn0 = _pool_layer(h, seq, l0_norm1, l0_pool_w_in, l0_pool_w_group, l0_pool_scale,
                          l0_pool_w_out, (l0_ffn_w_gate, l0_ffn_w_up, l0_ffn_w_down))
    h = _ffn(h, l0_norm2, *ffn0)
    h, ffn1 = _sgu_layer(h, l1_norm1, l1_sgu_w_in, l1_sgu_v_norm_w, l1_sgu_v_norm_b,
                         l1_sgu_w_spatial, l1_sgu_b_spatial, l1_sgu_w_out,
                         (l1_ffn_w_gate, l1_ffn_w_up, l1_ffn_w_down))
    h = _ffn(h, l1_norm2, *ffn1)
    h = _gla_layer(h, seq, l2_norm1, l2_gla_w_in, l2_gla_w_gk_up, l2_gla_b_gk,
                   l2_gla_out_norm_w, l2_gla_w_out)
    h = _ffn(h, l2_norm2, l2_ffn_w_gate, l2_ffn_w_up, l2_ffn_w_down)
    h, ffn3 = _pool_layer(h, seq, l3_norm1, l3_pool_w_in, l3_pool_w_group, l3_pool_scale,
                          l3_pool_w_out, (l3_ffn_w_gate, l3_ffn_w_up, l3_ffn_w_down))
    h = _ffn(h, l3_norm2, *ffn3, final_norm_w)
    return h.reshape(batch, seq, d)
```

```python
import functools

import jax
import jax.numpy as jnp
from jax import lax
from jax.experimental import pallas as pl
from jax.experimental.pallas import tpu as pltpu

F32 = jnp.float32
BF16 = jnp.bfloat16

EPS = 1e-6
POOL_WINDOWS = (2, 4, 8, 16)
POOL_HALO = 24
SGU_CHUNK = 128
GLA_HEADS = 4
GLA_CHUNK = 64
GLA_GATE_TEMP = 16.0
LANES = 128
VMEM_LIMIT = 58 * 1024 * 1024


def _params(*semantics):
    return pltpu.CompilerParams(dimension_semantics=semantics,
                                vmem_limit_bytes=VMEM_LIMIT)


def _tile(n, want):
    t = min(n, want)
    assert n % t == 0, (n, want)
    return t


def _rms_norm(x, w):
    ms = jnp.mean(x * x, axis=-1, keepdims=True)
    return x * lax.rsqrt(ms + EPS) * w


def _dot(a, b):
    return jnp.dot(a, b, preferred_element_type=F32)


def _const_spec(shape):
    return pl.BlockSpec(shape, lambda *_: (0,) * len(shape))


def _resident_spec(shape):
    return pl.BlockSpec(shape, lambda *_: (0,) * len(shape),
                        pipeline_mode=pl.Buffered(1))


BF16_SUBLANES = 16


def _cast_plan(steps, weights):
    in_specs, out_specs, out_shapes = [], [], []
    for w in weights:
        rows, cols = w.shape
        rep = 1
        while (rows * rep) % (steps * BF16_SUBLANES):
            rep *= 2
            assert rep <= steps
        blk = rows * rep // steps

        def index_map(i, *_, rep=rep):
            return (i // rep, 0)

        in_specs.append(pl.BlockSpec((blk, cols), index_map))
        out_specs.append(pl.BlockSpec((blk, cols), index_map))
        out_shapes.append(jax.ShapeDtypeStruct((rows, cols), BF16))
    return in_specs, out_specs, out_shapes


def _cast_blocks(src_refs, dst_refs):
    for src, dst in zip(src_refs, dst_refs):
        dst[...] = src[...].astype(BF16)


def _pool_layer_kernel(x_ref, nw_ref, win_ref, wg_ref, sc_ref, wout_ref, c0, c1, c2,
                       o_ref, d0, d1, d2, zbuf, *, tiles_per_seq, sub):
    _cast_blocks((c0, c1, c2), (d0, d1, d2))
    tm = x_ref.shape[0]
    cg = wg_ref.shape[1]
    pos = pl.program_id(0) % tiles_per_seq
    halo = POOL_HALO

    @pl.when(pos == 0)
    def _():
        zbuf[0:halo, :] = jnp.zeros((halo, zbuf.shape[1]), F32)

    @pl.when(pos != 0)
    def _():
        zbuf[0:halo, :] = zbuf[tm:tm + halo, :]

    ext = sub + 16
    for r0 in range(0, tm, sub):
        h = _rms_norm(x_ref[r0:r0 + sub, :], nw_ref[...]).astype(BF16)
        zbuf[halo + r0:halo + r0 + sub, :] = _dot(h, win_ref[...])
    for r0 in range(0, tm, sub):
        x = x_ref[r0:r0 + sub, :]
        t_idx = pos * tm + r0 + lax.broadcasted_iota(jnp.int32, (sub, cg), 0)
        ys = []
        for g, w in enumerate(POOL_WINDOWS):
            cols = slice(g * cg, (g + 1) * cg)
            z = zbuf[halo + r0:halo + r0 + sub, cols]
            s = zbuf[8 + r0:8 + r0 + ext, cols]
            shift = 1
            while shift < w:
                s = s + pltpu.roll(s, shift, 0)
                shift *= 2
            count = jnp.minimum(t_idx + 1, w).astype(F32)
            p = s[16:, :] / count - z
            y = _dot(p.astype(BF16), wg_ref[g]) * sc_ref[:, cols]
            ys.append(y.astype(BF16))
        y = jnp.concatenate(ys, axis=-1)
        o_ref[r0:r0 + sub, :] = x + _dot(y, wout_ref[...])


def _pool_layer(x, seq, norm_w, w_in, w_group, scale, w_out, ffn_weights, *, tm=512, sub=256):
    t, d = x.shape
    p = w_in.shape[1]
    g, cg, _ = w_group.shape
    tm = _tile(seq, tm)
    sub = _tile(tm, sub)
    cast_in, cast_out, cast_shapes = _cast_plan(t // tm, ffn_weights)
    out, *ffn_bf16 = pl.pallas_call(
        functools.partial(_pool_layer_kernel, tiles_per_seq=seq // tm, sub=sub),
        grid=(t // tm,),
        in_specs=[pl.BlockSpec((tm, d), lambda i: (i, 0)),
                  _resident_spec((1, d)),
                  _resident_spec((d, p)),
                  _resident_spec((g, cg, cg)),
                  _resident_spec((1, p)),
                  _resident_spec((p, d))] + cast_in,
        out_specs=[pl.BlockSpec((tm, d), lambda i: (i, 0))] + cast_out,
        out_shape=[jax.ShapeDtypeStruct((t, d), F32)] + cast_shapes,
        scratch_shapes=[pltpu.VMEM((tm + POOL_HALO, p), F32)],
        compiler_params=_params("arbitrary"),
        name="pool_layer",
    )(x, norm_w.reshape(1, d), w_in.astype(BF16), w_group.astype(BF16),
      scale.reshape(1, p), w_out.astype(BF16), *ffn_weights)
    return out, ffn_bf16


FFN_NORM_STEPS = 4


def _ffn_kernel(x_hbm, nw_ref, wg_ref, wu_ref, wd_ref, *rest, n_cast, final_norm):
    rest = list(rest)
    fnw_ref = rest.pop(0) if final_norm else None
    cast_src = [rest.pop(0) for _ in range(n_cast)]
    o_hbm = rest.pop(0)
    cast_dst = [rest.pop(0) for _ in range(n_cast)]
    acc0, acc1, hbuf0, hbuf1, in_sem, out_sem = rest
    acc, hbuf = (acc0, acc1), (hbuf0, hbuf1)
    i, j = pl.program_id(0), pl.program_id(1)
    nt, nj = pl.num_programs(0), pl.num_programs(1)
    tm = acc0.shape[0]
    rq = tm // FFN_NORM_STEPS
    q = j - (nj - FFN_NORM_STEPS)
    has_next = i + 1 < nt

    def x_copy(tile, s):
        return pltpu.make_async_copy(x_hbm.at[pl.ds(tile * tm, tm), :], acc[s], in_sem.at[s])

    def o_copy(tile, s):
        return pltpu.make_async_copy(acc[s], o_hbm.at[pl.ds(tile * tm, tm), :], out_sem.at[s])

    if n_cast:
        @pl.when(j == 0)
        def _():
            _cast_blocks(cast_src, cast_dst)

    @pl.when((i == 0) & (j == 0))
    def _():
        first = x_copy(0, 0)
        first.start()
        first.wait()
        hbuf[0][...] = _rms_norm(acc[0][...], nw_ref[...]).astype(BF16)

    def step(slot, with_norm):
        other = 1 - slot
        h = hbuf[slot][...]
        tfh = wg_ref.shape[1] // 2
        rp = rq // 4

        def norm_piece(p):
            if with_norm:
                rows = pl.ds(pl.multiple_of(q * rq + p * rp, rp), rp)
                hbuf[other][rows, :] = _rms_norm(acc[other][rows, :], nw_ref[...]).astype(BF16)

        parts = []
        for half in range(2):
            cols = slice(half * tfh, (half + 1) * tfh)
            g = _dot(h, wg_ref[:, cols].astype(BF16))
            norm_piece(2 * half)
            u = _dot(h, wu_ref[:, cols].astype(BF16))
            norm_piece(2 * half + 1)
            parts.append((g * jax.nn.sigmoid(g) * u).astype(BF16))
        a = jnp.concatenate(parts, axis=-1)
        acc[slot][...] += _dot(a, wd_ref[...].astype(BF16))

    for slot in (0, 1):
        other = 1 - slot
        mine = (i % 2) == slot

        @pl.when(mine & (j == 1) & (i >= 1))
        def _():
            o_copy(i - 1, other).wait()

        @pl.when(mine & (j == 1) & has_next)
        def _():
            x_copy(i + 1, other).start()

        @pl.when(mine & (q == 0) & has_next)
        def _():
            x_copy(i + 1, other).wait()

        norm_next = (q >= 0) & has_next
        pl.when(mine & norm_next)(functools.partial(step, slot, True))
        pl.when(mine & jnp.logical_not(norm_next))(functools.partial(step, slot, False))

        @pl.when(mine & (j == nj - 1))
        def _():
            if fnw_ref is not None:
                acc[slot][...] = _rms_norm(acc[slot][...], fnw_ref[...])
            o_copy(i, slot).start()

        @pl.when(mine & (j == nj - 1) & (i == nt - 1))
        def _():
            o_copy(i, slot).wait()


def _ffn(x, norm_w, w_gate, w_up, w_down, final_norm_w=None, cast_weights=(),
         *, tm=1024, tf=512):
    t, d = x.shape
    f = w_gate.shape[1]
    tm, tf = _tile(t, tm), _tile(f, tf)
    nt, nj = t // tm, f // tf
    assert nt >= 2 and nj >= FFN_NORM_STEPS + 2 and tm % FFN_NORM_STEPS == 0
    in_specs = [pl.BlockSpec(memory_space=pl.ANY),
                _const_spec((1, d)),
                pl.BlockSpec((d, tf), lambda i, j: (0, j)),
                pl.BlockSpec((d, tf), lambda i, j: (0, j)),
                pl.BlockSpec((tf, d), lambda i, j: (j, 0))]
    args = [x, norm_w.reshape(1, d), w_gate, w_up, w_down]
    if final_norm_w is not None:
        in_specs.append(_const_spec((1, d)))
        args.append(final_norm_w.reshape(1, d))
    cast_in, cast_out, cast_shapes = _cast_plan(nt, cast_weights)
    out, *cast_bf16 = pl.pallas_call(
        functools.partial(_ffn_kernel, n_cast=len(cast_weights),
                          final_norm=final_norm_w is not None),
        grid=(nt, nj),
        in_specs=in_specs + cast_in,
        out_specs=[pl.BlockSpec(memory_space=pl.ANY)] + cast_out,
        out_shape=[jax.ShapeDtypeStruct((t, d), F32)] + cast_shapes,
        scratch_shapes=[pltpu.VMEM((tm, d), F32), pltpu.VMEM((tm, d), F32),
                        pltpu.VMEM((tm, d), BF16), pltpu.VMEM((tm, d), BF16),
                        pltpu.SemaphoreType.DMA((2,)),
                        pltpu.SemaphoreType.DMA((2,))],
        compiler_params=_params("arbitrary", "arbitrary"),
        name="ffn",
    )(*args, *cast_weights)
    return out, cast_bf16


def _gelu(z):
    return 0.5 * z * (1.0 + lax.erf(z * (2.0 ** -0.5)))


def _sgu_layer_kernel(x_ref, nw_ref, win_ref, lnw_ref, lnb_ref, ws_ref, bs_ref, wout_ref,
                      c0, c1, c2, o_ref, d0, d1, d2, *, sub):
    _cast_blocks((c0, c1, c2), (d0, d1, d2))
    tm = x_ref.shape[0]
    dim = lnw_ref.shape[1]
    heads = ws_ref.shape[0]
    hd = dim // heads
    c = SGU_CHUNK

    row = lax.broadcasted_iota(jnp.int32, (c, c), 0)
    col = lax.broadcasted_iota(jnp.int32, (c, c), 1)
    causal = col <= row
    wms = [jnp.where(causal, ws_ref[h], 0.0).astype(BF16) for h in range(heads)]

    for r0 in range(0, tm, sub):
        x = x_ref[r0:r0 + sub, :]
        h = _rms_norm(x, nw_ref[...]).astype(BF16)
        v = _gelu(_dot(h, win_ref[:, dim:]))
        mu = jnp.mean(v, axis=-1, keepdims=True)
        vc = v - mu
        var = jnp.mean(vc * vc, axis=-1, keepdims=True)
        vn = (vc * lax.rsqrt(var + EPS) * lnw_ref[...] + lnb_ref[...]).astype(BF16)
        parts = []
        for hh in range(heads):
            cols = slice(hh * hd, (hh + 1) * hd)
            u = _gelu(_dot(h, win_ref[:, cols]))
            bias = bs_ref[:, hh:hh + 1]
            gate = jnp.concatenate(
                [_dot(wms[hh], vn[n * c:(n + 1) * c, cols]) + bias for n in range(sub // c)],
                axis=0)
            parts.append((u * gate).astype(BF16))
        a = jnp.concatenate(parts, axis=-1)
        o_ref[r0:r0 + sub, :] = x + _dot(a, wout_ref[...])


def _sgu_layer(x, norm_w, w_in, ln_w, ln_b, w_spatial, b_spatial, w_out, ffn_weights,
               *, tm=512, sub=256):
    t, d = x.shape
    dim = w_out.shape[0]
    heads, c, _ = w_spatial.shape
    tm = _tile(t, tm)
    sub = _tile(tm, sub)
    cast_in, cast_out, cast_shapes = _cast_plan(t // tm, ffn_weights)
    out, *ffn_bf16 = pl.pallas_call(
        functools.partial(_sgu_layer_kernel, sub=sub),
        grid=(t // tm,),
        in_specs=[pl.BlockSpec((tm, d), lambda i: (i, 0)),
                  _resident_spec((1, d)),
                  _resident_spec((d, 2 * dim)),
                  _resident_spec((1, dim)),
                  _resident_spec((1, dim)),
                  _resident_spec((heads, c, c)),
                  _resident_spec((c, heads)),
                  _resident_spec((dim, d))] + cast_in,
        out_specs=[pl.BlockSpec((tm, d), lambda i: (i, 0))] + cast_out,
        out_shape=[jax.ShapeDtypeStruct((t, d), F32)] + cast_shapes,
        compiler_params=_params("arbitrary"),
        name="sgu_layer",
    )(x, norm_w.reshape(1, d), w_in.astype(BF16), ln_w.reshape(1, dim), ln_b.reshape(1, dim),
      w_spatial, b_spatial.T, w_out.astype(BF16), *ffn_weights)
    return out, ffn_bf16


def _gla_layer_kernel(x_ref, nw_ref, w_ref, wlo_ref, wup_ref, bgk_ref, onw_ref, wout_ref,
                      o_ref, st_ref, *, tiles_per_seq):
    tm = x_ref.shape[0]
    dkt = wup_ref.shape[1]
    dvt = onw_ref.shape[1]
    hk, hv = dkt // GLA_HEADS, dvt // GLA_HEADS
    c = GLA_CHUNK
    heads = range(GLA_HEADS)

    @pl.when(pl.program_id(0) % tiles_per_seq == 0)
    def _():
        st_ref[...] = jnp.zeros(st_ref.shape, F32)

    row_k = lax.broadcasted_iota(jnp.int32, (2 * c, hk), 0)
    in_a = row_k < c
    row_in_chunk = jnp.where(in_a, row_k, row_k - c)
    row = lax.broadcasted_iota(jnp.int32, (c, c), 0)
    col = lax.broadcasted_iota(jnp.int32, (c, c), 1)
    causal = col <= row
    nt = (((1,), (1,)), ((), ()))
    tn = (((0,), (0,)), ((), ()))

    x = x_ref[...]
    hn = _rms_norm(x, nw_ref[...]).astype(BF16)
    low = _dot(hn, wlo_ref[...]).astype(BF16)
    q_all = _dot(hn, w_ref[:, 0:dkt]) * (hk ** -0.5)
    k_all = _dot(hn, w_ref[:, dkt:2 * dkt])
    g_all = jax.nn.log_sigmoid(_dot(low, wup_ref[...]) + bgk_ref[...]) / GLA_GATE_TEMP
    v_all = _dot(hn, w_ref[:, 2 * dkt:2 * dkt + dvt]).astype(BF16)
    r_all = _dot(hn, w_ref[:, 2 * dkt + dvt:])
    gate_all = r_all * jax.nn.sigmoid(r_all) * onw_ref[...]

    row_parts = []
    for s in range(tm // (2 * c)):
        rows = slice(s * 2 * c, (s + 1) * 2 * c)
        rows_a = slice(s * 2 * c, s * 2 * c + c)
        q_in, k_in, k_dec_a, q2, k2, dc_pair = [], [], [], [], [], []
        for h in heads:
            kc = slice(h * hk, (h + 1) * hk)
            b = g_all[rows, kc]
            shift = 1
            while shift < c:
                b = b + jnp.where(row_in_chunk >= shift, pltpu.roll(b, shift, 0), 0.0)
                shift *= 2
            b_last_a, b_last_b = b[c - 1:c, :], b[2 * c - 1:2 * c, :]
            dc_a, dc_b = jnp.exp(b_last_a), jnp.exp(b_last_b)
            k = k_all[rows, kc]
            qi = q_all[rows, kc] * jnp.exp(b)
            kd = k * jnp.exp(jnp.where(in_a, b_last_a, b_last_b) - b)
            q_in.append(qi.astype(BF16))
            k_in.append((k * jnp.exp(-b)).astype(BF16))
            k_dec_a.append(kd[:c].astype(BF16))
            q2.append(jnp.where(in_a, qi, qi * dc_a).astype(BF16))
            k2.append(jnp.where(in_a, kd * dc_b, kd).astype(BF16))
            dc_pair.append(dc_a * dc_b)
        s_aa = [lax.dot_general(q_in[h][:c], k_in[h][:c], nt, preferred_element_type=F32)
                for h in heads]
        s_bb = [lax.dot_general(q_in[h][c:], k_in[h][c:], nt, preferred_element_type=F32)
                for h in heads]
        s_ba = [lax.dot_general(q_in[h][c:], k_dec_a[h], nt, preferred_element_type=F32)
                for h in heads]
        p_a = [jnp.where(causal, s_aa[h], 0.0).astype(BF16) for h in heads]
        p_b = [jnp.concatenate([s_ba[h], jnp.where(causal, s_bb[h], 0.0)], axis=1).astype(BF16)
               for h in heads]
        o = []
        for h in heads:
            vc = slice(h * hv, (h + 1) * hv)
            intra = jnp.concatenate([_dot(p_a[h], v_all[rows_a, vc]),
                                     _dot(p_b[h], v_all[rows, vc])], axis=0)
            inter = lax.dot_general(q2[h], st_ref[h].astype(BF16), nt,
                                    preferred_element_type=F32)
            o.append(intra + inter)
        for h in heads:
            vc = slice(h * hv, (h + 1) * hv)
            st_ref[h] = dc_pair[h] * st_ref[h] + lax.dot_general(
                v_all[rows, vc], k2[h], tn, preferred_element_type=F32)
        parts = []
        for h in heads:
            vc = slice(h * hv, (h + 1) * hv)
            on = o[h] * lax.rsqrt(jnp.mean(o[h] * o[h], axis=-1, keepdims=True) + EPS)
            parts.append((on * gate_all[rows, vc]).astype(BF16))
        row_parts.append(jnp.concatenate(parts, axis=-1))
    a = jnp.concatenate(row_parts, axis=0)
    o_ref[...] = x + _dot(a, wout_ref[...])


def _gla_layer(x, seq, norm_w, w_in, w_gk_up, b_gk, out_norm_w, w_out, *, tm=256):
    t, d = x.shape
    rank, dk = w_gk_up.shape
    dv = out_norm_w.shape[0]
    tm = _tile(seq, tm)
    hk, hv = dk // GLA_HEADS, dv // GLA_HEADS
    n_main = 2 * dk + 2 * dv
    assert n_main % LANES == 0
    rank_pad = -(-rank // LANES) * LANES
    w_main = w_in.astype(BF16)
    w_low = jnp.pad(w_in[:, n_main:], ((0, 0), (0, rank_pad - rank))).astype(BF16)
    w_up = jnp.pad(w_gk_up, ((0, rank_pad - rank), (0, 0))).astype(BF16)
    return pl.pallas_call(
        functools.partial(_gla_layer_kernel, tiles_per_seq=seq // tm),
        grid=(t // tm,),
        in_specs=[pl.BlockSpec((tm, d), lambda i: (i, 0)),
                  _resident_spec((1, d)),
                  _resident_spec((d, n_main)),
                  _resident_spec((d, rank_pad)),
                  _resident_spec((rank_pad, dk)),
                  _resident_spec((1, dk)),
                  _resident_spec((1, dv)),
                  _resident_spec((dv, d))],
        out_specs=pl.BlockSpec((tm, d), lambda i: (i, 0)),
        out_shape=jax.ShapeDtypeStruct((t, d), F32),
        scratch_shapes=[pltpu.VMEM((GLA_HEADS, hv, hk), F32)],
        compiler_params=_params("arbitrary"),
        name="gla_layer",
    )(x, norm_w.reshape(1, d), w_main, w_low, w_up, b_gk.reshape(1, dk),
      out_norm_w.reshape(1, dv), w_out.astype(BF16))


def kernel(x, l0_norm1, l0_pool_w_in, l0_pool_w_group, l0_pool_scale, l0_pool_w_out, l0_norm2, l0_ffn_w_gate, l0_ffn_w_up, l0_ffn_w_down, l1_norm1, l1_sgu_w_in, l1_sgu_v_norm_w, l1_sgu_v_norm_b, l1_sgu_w_spatial, l1_sgu_b_spatial, l1_sgu_w_out, l1_norm2, l1_ffn_w_gate, l1_ffn_w_up, l1_ffn_w_down, l2_norm1, l2_gla_w_in, l2_gla_w_gk_up, l2_gla_b_gk, l2_gla_out_norm_w, l2_gla_w_out, l2_norm2, l2_ffn_w_gate, l2_ffn_w_up, l2_ffn_w_down, l3_norm1, l3_pool_w_in, l3_pool_w_group, l3_pool_scale, l3_pool_w_out, l3_norm2, l3_ffn_w_gate, l3_ffn_w_up, l3_ffn_w_down, final_norm_w):
    batch, seq, d = x.shape
    h = x.reshape(batch * seq, d)
    h, ffn0 = _pool_layer(h, seq, l0_norm1, l0_pool_w_in, l0_pool_w_group, l0_pool_scale,
                          l0_pool_w_out, (l0_ffn_w_gate, l0_ffn_w_up, l0_ffn_w_down))
    h, (sgu_w_in, sgu_w_out) = _ffn(h, l0_norm2, *ffn0,
                                    cast_weights=(l1_sgu_w_in, l1_sgu_w_out))
    h, ffn1 = _sgu_layer(h, l1_norm1, sgu_w_in, l1_sgu_v_norm_w, l1_sgu_v_norm_b,
                         l1_sgu_w_spatial, l1_sgu_b_spatial, sgu_w_out,
                         (l1_ffn_w_gate, l1_ffn_w_up, l1_ffn_w_down))
    h, (gla_w_in, gla_w_out) = _ffn(h, l1_norm2, *ffn1,
                                    cast_weights=(l2_gla_w_in, l2_gla_w_out))
    h = _gla_layer(h, seq, l2_norm1, gla_w_in, l2_gla_w_gk_up, l2_gla_b_gk,
                   l2_gla_out_norm_w, gla_w_out)
    group_shape = l3_pool_w_group.shape
    h, (pool_w_in, pool_w_group, pool_w_out) = _ffn(
        h, l2_norm2, l2_ffn_w_gate, l2_ffn_w_up, l2_ffn_w_down,
        cast_weights=(l3_pool_w_in, l3_pool_w_group.reshape(-1, group_shape[-1]),
                      l3_pool_w_out))
    h, ffn3 = _pool_layer(h, seq, l3_norm1, pool_w_in, pool_w_group.reshape(group_shape),
                          l3_pool_scale, pool_w_out,
                          (l3_ffn_w_gate, l3_ffn_w_up, l3_ffn_w_down))
    h, _ = _ffn(h, l3_norm2, *ffn3, final_norm_w)
    return h.reshape(batch, seq, d)
```

```python
import functools

import jax
import jax.numpy as jnp
from jax import lax
from jax.experimental import pallas as pl
from jax.experimental.pallas import tpu as pltpu

F32 = jnp.float32
BF16 = jnp.bfloat16

EPS = 1e-6
POOL_WINDOWS = (2, 4, 8, 16)
POOL_HALO = 24
SGU_CHUNK = 128
GLA_HEADS = 4
GLA_CHUNK = 64
GLA_GATE_TEMP = 16.0
LANES = 128
VMEM_LIMIT = 58 * 1024 * 1024


def _params(*semantics):
    return pltpu.CompilerParams(dimension_semantics=semantics,
                                vmem_limit_bytes=VMEM_LIMIT)


def _tile(n, want):
    t = min(n, want)
    assert n % t == 0, (n, want)
    return t


def _rms_norm(x, w):
    ms = jnp.mean(x * x, axis=-1, keepdims=True)
    return x * lax.rsqrt(ms + EPS) * w


def _dot(a, b):
    return jnp.dot(a, b, preferred_element_type=F32)


def _const_spec(shape):
    return pl.BlockSpec(shape, lambda *_: (0,) * len(shape))


def _resident_spec(shape):
    return pl.BlockSpec(shape, lambda *_: (0,) * len(shape),
                        pipeline_mode=pl.Buffered(1))


BF16_SUBLANES = 16


def _cast_plan(steps, weights):
    in_specs, out_specs, out_shapes = [], [], []
    for w in weights:
        rows, cols = w.shape
        rep = 1
        while (rows * rep) % (steps * BF16_SUBLANES):
            rep *= 2
            assert rep <= steps
        blk = rows * rep // steps

        def index_map(i, *_, rep=rep):
            return (i // rep, 0)

        in_specs.append(pl.BlockSpec((blk, cols), index_map))
        out_specs.append(pl.BlockSpec((blk, cols), index_map))
        out_shapes.append(jax.ShapeDtypeStruct((rows, cols), BF16))
    return in_specs, out_specs, out_shapes


def _cast_blocks(src_refs, dst_refs):
    for src, dst in zip(src_refs, dst_refs):
        dst[...] = src[...].astype(BF16)


def _pool_layer_kernel(x_ref, nw_ref, win_ref, wg_ref, sc_ref, wout_ref, c0, c1, c2,
                       o_ref, d0, d1, d2, zbuf, *, tiles_per_seq, sub):
    _cast_blocks((c0, c1, c2), (d0, d1, d2))
    tm = x_ref.shape[0]
    cg = wg_ref.shape[1]
    pos = pl.program_id(0) % tiles_per_seq
    halo = POOL_HALO

    @pl.when(pos == 0)
    def _():
        zbuf[0:halo, :] = jnp.zeros((halo, zbuf.shape[1]), F32)

    @pl.when(pos != 0)
    def _():
        zbuf[0:halo, :] = zbuf[tm:tm + halo, :]

    ext = sub + 16
    for r0 in range(0, tm, sub):
        h = _rms_norm(x_ref[r0:r0 + sub, :], nw_ref[...]).astype(BF16)
        zbuf[halo + r0:halo + r0 + sub, :] = _dot(h, win_ref[...])
    for r0 in range(0, tm, sub):
        x = x_ref[r0:r0 + sub, :]
        t_idx = pos * tm + r0 + lax.broadcasted_iota(jnp.int32, (sub, cg), 0)
        ys = []
        for g, w in enumerate(POOL_WINDOWS):
            cols = slice(g * cg, (g + 1) * cg)
            z = zbuf[halo + r0:halo + r0 + sub, cols]
            s = zbuf[8 + r0:8 + r0 + ext, cols]
            shift = 1
            while shift < w:
                s = s + pltpu.roll(s, shift, 0)
                shift *= 2
            count = jnp.minimum(t_idx + 1, w).astype(F32)
            p = s[16:, :] / count - z
            y = _dot(p.astype(BF16), wg_ref[g]) * sc_ref[:, cols]
            ys.append(y.astype(BF16))
        y = jnp.concatenate(ys, axis=-1)
        o_ref[r0:r0 + sub, :] = x + _dot(y, wout_ref[...])


def _pool_layer(x, seq, norm_w, w_in, w_group, scale, w_out, ffn_weights, *, tm=512, sub=256):
    t, d = x.shape
    p = w_in.shape[1]
    g, cg, _ = w_group.shape
    tm = _tile(seq, tm)
    sub = _tile(tm, sub)
    cast_in, cast_out, cast_shapes = _cast_plan(t // tm, ffn_weights)
    out, *ffn_bf16 = pl.pallas_call(
        functools.partial(_pool_layer_kernel, tiles_per_seq=seq // tm, sub=sub),
        grid=(t // tm,),
        in_specs=[pl.BlockSpec((tm, d), lambda i: (i, 0)),
                  _resident_spec((1, d)),
                  _resident_spec((d, p)),
                  _resident_spec((g, cg, cg)),
                  _resident_spec((1, p)),
                  _resident_spec((p, d))] + cast_in,
        out_specs=[pl.BlockSpec((tm, d), lambda i: (i, 0))] + cast_out,
        out_shape=[jax.ShapeDtypeStruct((t, d), F32)] + cast_shapes,
        scratch_shapes=[pltpu.VMEM((tm + POOL_HALO, p), F32)],
        compiler_params=_params("arbitrary"),
        name="pool_layer",
    )(x, norm_w.reshape(1, d), w_in.astype(BF16), w_group.astype(BF16),
      scale.reshape(1, p), w_out.astype(BF16), *ffn_weights)
    return out, ffn_bf16


FFN_NORM_STEPS = 4


def _ffn_kernel(x_hbm, nw_ref, wg_ref, wu_ref, wd_ref, *rest, n_cast, final_norm):
    rest = list(rest)
    fnw_ref = rest.pop(0) if final_norm else None
    cast_src = [rest.pop(0) for _ in range(n_cast)]
    o_hbm = rest.pop(0)
    cast_dst = [rest.pop(0) for _ in range(n_cast)]
    acc0, acc1, hbuf0, hbuf1, in_sem, out_sem = rest
    acc, hbuf = (acc0, acc1), (hbuf0, hbuf1)
    i, j = pl.program_id(0), pl.program_id(1)
    nt, nj = pl.num_programs(0), pl.num_programs(1)
    tm = acc0.shape[0]
    rq = tm // FFN_NORM_STEPS
    q = j - (nj - FFN_NORM_STEPS)
    has_next = i + 1 < nt

    def x_copy(tile, s):
        return pltpu.make_async_copy(x_hbm.at[pl.ds(tile * tm, tm), :], acc[s], in_sem.at[s])

    def o_copy(tile, s):
        return pltpu.make_async_copy(acc[s], o_hbm.at[pl.ds(tile * tm, tm), :], out_sem.at[s])

    if n_cast:
        @pl.when(j == 0)
        def _():
            _cast_blocks(cast_src, cast_dst)

    @pl.when((i == 0) & (j == 0))
    def _():
        first = x_copy(0, 0)
        first.start()
        first.wait()
        hbuf[0][...] = _rms_norm(acc[0][...], nw_ref[...]).astype(BF16)

    def step(slot, with_norm):
        other = 1 - slot
        h = hbuf[slot][...]
        tfh = wg_ref.shape[1] // 2
        rp = rq // 4

        def norm_piece(p):
            if with_norm:
                rows = pl.ds(pl.multiple_of(q * rq + p * rp, rp), rp)
                hbuf[other][rows, :] = _rms_norm(acc[other][rows, :], nw_ref[...]).astype(BF16)

        parts = []
        for half in range(2):
            cols = slice(half * tfh, (half + 1) * tfh)
            g = _dot(h, wg_ref[:, cols].astype(BF16))
            norm_piece(2 * half)
            u = _dot(h, wu_ref[:, cols].astype(BF16))
            norm_piece(2 * half + 1)
            parts.append((g * jax.nn.sigmoid(g) * u).astype(BF16))
        a = jnp.concatenate(parts, axis=-1)
        acc[slot][...] += _dot(a, wd_ref[...].astype(BF16))

    for slot in (0, 1):
        other = 1 - slot
        mine = (i % 2) == slot

        @pl.when(mine & (j == 1) & (i >= 1))
        def _():
            o_copy(i - 1, other).wait()

        @pl.when(mine & (j == 1) & has_next)
        def _():
            x_copy(i + 1, other).start()

        @pl.when(mine & (q == 0) & has_next)
        def _():
            x_copy(i + 1, other).wait()

        norm_next = (q >= 0) & has_next
        pl.when(mine & norm_next)(functools.partial(step, slot, True))
        pl.when(mine & jnp.logical_not(norm_next))(functools.partial(step, slot, False))

        @pl.when(mine & (j == nj - 1))
        def _():
            if fnw_ref is not None:
                acc[slot][...] = _rms_norm(acc[slot][...], fnw_ref[...])
            o_copy(i, slot).start()

        @pl.when(mine & (j == nj - 1) & (i == nt - 1))
        def _():
            o_copy(i, slot).wait()


def _ffn(x, norm_w, w_gate, w_up, w_down, final_norm_w=None, cast_weights=(),
         *, tm=1024, tf=512):
    t, d = x.shape
    f = w_gate.shape[1]
    tm, tf = _tile(t, tm), _tile(f, tf)
    nt, nj = t // tm, f // tf
    assert nt >= 2 and nj >= FFN_NORM_STEPS + 2 and tm % FFN_NORM_STEPS == 0
    in_specs = [pl.BlockSpec(memory_space=pl.ANY),
                _const_spec((1, d)),
                pl.BlockSpec((d, tf), lambda i, j: (0, j)),
                pl.BlockSpec((d, tf), lambda i, j: (0, j)),
                pl.BlockSpec((tf, d), lambda i, j: (j, 0))]
    args = [x, norm_w.reshape(1, d), w_gate, w_up, w_down]
    if final_norm_w is not None:
        in_specs.append(_const_spec((1, d)))
        args.append(final_norm_w.reshape(1, d))
    cast_in, cast_out, cast_shapes = _cast_plan(nt, cast_weights)
    out, *cast_bf16 = pl.pallas_call(
        functools.partial(_ffn_kernel, n_cast=len(cast_weights),
                          final_norm=final_norm_w is not None),
        grid=(nt, nj),
        in_specs=in_specs + cast_in,
        out_specs=[pl.BlockSpec(memory_space=pl.ANY)] + cast_out,
        out_shape=[jax.ShapeDtypeStruct((t, d), F32)] + cast_shapes,
        scratch_shapes=[pltpu.VMEM((tm, d), F32), pltpu.VMEM((tm, d), F32),
                        pltpu.VMEM((tm, d), BF16), pltpu.VMEM((tm, d), BF16),
                        pltpu.SemaphoreType.DMA((2,)),
                        pltpu.SemaphoreType.DMA((2,))],
        compiler_params=_params("arbitrary", "arbitrary"),
        name="ffn",
    )(*args, *cast_weights)
    return out, cast_bf16


def _gelu(z):
    return 0.5 * z * (1.0 + lax.erf(z * (2.0 ** -0.5)))


def _sgu_layer_kernel(x_ref, nw_ref, win_ref, lnw_ref, lnb_ref, ws_ref, bs_ref, wout_ref,
                      c0, c1, c2, o_ref, d0, d1, d2, *, sub):
    _cast_blocks((c0, c1, c2), (d0, d1, d2))
    tm = x_ref.shape[0]
    dim = lnw_ref.shape[1]
    heads = ws_ref.shape[0]
    hd = dim // heads
    c = SGU_CHUNK

    row = lax.broadcasted_iota(jnp.int32, (c, c), 0)
    col = lax.broadcasted_iota(jnp.int32, (c, c), 1)
    causal = col <= row
    wms = [jnp.where(causal, ws_ref[h], 0.0).astype(BF16) for h in range(heads)]

    for r0 in range(0, tm, sub):
        x = x_ref[r0:r0 + sub, :]
        h = _rms_norm(x, nw_ref[...]).astype(BF16)
        v = _gelu(_dot(h, win_ref[:, dim:]))
        mu = jnp.mean(v, axis=-1, keepdims=True)
        vc = v - mu
        var = jnp.mean(vc * vc, axis=-1, keepdims=True)
        vn = (vc * lax.rsqrt(var + EPS) * lnw_ref[...] + lnb_ref[...]).astype(BF16)
        parts = []
        for hh in range(heads):
            cols = slice(hh * hd, (hh + 1) * hd)
            u = _gelu(_dot(h, win_ref[:, cols]))
            bias = bs_ref[:, hh:hh + 1]
            gate = jnp.concatenate(
                [_dot(wms[hh], vn[n * c:(n + 1) * c, cols]) + bias for n in range(sub // c)],
                axis=0)
            parts.append((u * gate).astype(BF16))
        a = jnp.concatenate(parts, axis=-1)
        o_ref[r0:r0 + sub, :] = x + _dot(a, wout_ref[...])


def _sgu_layer(x, norm_w, w_in, ln_w, ln_b, w_spatial, b_spatial, w_out, ffn_weights,
               *, tm=512, sub=256):
    t, d = x.shape
    dim = w_out.shape[0]
    heads, c, _ = w_spatial.shape
    tm = _tile(t, tm)
    sub = _tile(tm, sub)
    cast_in, cast_out, cast_shapes = _cast_plan(t // tm, ffn_weights)
    out, *ffn_bf16 = pl.pallas_call(
        functools.partial(_sgu_layer_kernel, sub=sub),
        grid=(t // tm,),
        in_specs=[pl.BlockSpec((tm, d), lambda i: (i, 0)),
                  _resident_spec((1, d)),
                  _resident_spec((d, 2 * dim)),
                  _resident_spec((1, dim)),
                  _resident_spec((1, dim)),
                  _resident_spec((heads, c, c)),
                  _resident_spec((c, heads)),
                  _resident_spec((dim, d))] + cast_in,
        out_specs=[pl.BlockSpec((tm, d), lambda i: (i, 0))] + cast_out,
        out_shape=[jax.ShapeDtypeStruct((t, d), F32)] + cast_shapes,
        compiler_params=_params("arbitrary"),
        name="sgu_layer",
    )(x, norm_w.reshape(1, d), w_in.astype(BF16), ln_w.reshape(1, dim), ln_b.reshape(1, dim),
      w_spatial, b_spatial.T, w_out.astype(BF16), *ffn_weights)
    return out, ffn_bf16


def _gla_layer_kernel(x_ref, nw_ref, w_ref, wlo_ref, wup_ref, bgk_ref, onw_ref, wout_ref,
                      o_ref, st_ref, *, tiles_per_seq):
    tm = x_ref.shape[0]
    dkt = wup_ref.shape[1]
    dvt = onw_ref.shape[1]
    hk, hv = dkt // GLA_HEADS, dvt // GLA_HEADS
    c = GLA_CHUNK
    heads = range(GLA_HEADS)

    @pl.when(pl.program_id(0) % tiles_per_seq == 0)
    def _():
        st_ref[...] = jnp.zeros(st_ref.shape, F32)

    row_k = lax.broadcasted_iota(jnp.int32, (2 * c, hk), 0)
    in_a = row_k < c
    row_in_chunk = jnp.where(in_a, row_k, row_k - c)
    row = lax.broadcasted_iota(jnp.int32, (c, c), 0)
    col = lax.broadcasted_iota(jnp.int32, (c, c), 1)
    causal = col <= row
    nt = (((1,), (1,)), ((), ()))
    tn = (((0,), (0,)), ((), ()))

    x = x_ref[...]
    hn = _rms_norm(x, nw_ref[...]).astype(BF16)
    low = _dot(hn, wlo_ref[...]).astype(BF16)
    q_all = _dot(hn, w_ref[:, 0:dkt]) * (hk ** -0.5)
    k_all = _dot(hn, w_ref[:, dkt:2 * dkt])
    g_all = jax.nn.log_sigmoid(_dot(low, wup_ref[...]) + bgk_ref[...]) / GLA_GATE_TEMP
    v_all = _dot(hn, w_ref[:, 2 * dkt:2 * dkt + dvt]).astype(BF16)
    r_all = _dot(hn, w_ref[:, 2 * dkt + dvt:])
    gate_all = r_all * jax.nn.sigmoid(r_all) * onw_ref[...]

    pairs = range(tm // (2 * c))
    rows_of = [slice(s * 2 * c, (s + 1) * 2 * c) for s in pairs]
    units = [(s, h) for s in pairs for h in heads]
    q_in, k_in, k_dec_a, q2, k2, dc_pair = {}, {}, {}, {}, {}, {}
    for s, h in units:
        rows = rows_of[s]
        kc = slice(h * hk, (h + 1) * hk)
        b = g_all[rows, kc]
        shift = 1
        while shift < c:
            b = b + jnp.where(row_in_chunk >= shift, pltpu.roll(b, shift, 0), 0.0)
            shift *= 2
        b_last_a, b_last_b = b[c - 1:c, :], b[2 * c - 1:2 * c, :]
        dc_a, dc_b = jnp.exp(b_last_a), jnp.exp(b_last_b)
        k = k_all[rows, kc]
        qi = q_all[rows, kc] * jnp.exp(b)
        kd = k * jnp.exp(jnp.where(in_a, b_last_a, b_last_b) - b)
        q_in[s, h] = qi.astype(BF16)
        k_in[s, h] = (k * jnp.exp(-b)).astype(BF16)
        k_dec_a[s, h] = kd[:c].astype(BF16)
        q2[s, h] = jnp.where(in_a, qi, qi * dc_a).astype(BF16)
        k2[s, h] = jnp.where(in_a, kd * dc_b, kd).astype(BF16)
        dc_pair[s, h] = dc_a * dc_b
    s_aa = {u: lax.dot_general(q_in[u][:c], k_in[u][:c], nt, preferred_element_type=F32)
            for u in units}
    s_bb = {u: lax.dot_general(q_in[u][c:], k_in[u][c:], nt, preferred_element_type=F32)
            for u in units}
    s_ba = {u: lax.dot_general(q_in[u][c:], k_dec_a[u], nt, preferred_element_type=F32)
            for u in units}
    p_a = {u: jnp.where(causal, s_aa[u], 0.0).astype(BF16) for u in units}
    p_b = {u: jnp.concatenate([s_ba[u], jnp.where(causal, s_bb[u], 0.0)], axis=1).astype(BF16)
           for u in units}
    intra = {}
    for s, h in units:
        vc = slice(h * hv, (h + 1) * hv)
        rows_a = slice(s * 2 * c, s * 2 * c + c)
        intra[s, h] = jnp.concatenate([_dot(p_a[s, h], v_all[rows_a, vc]),
                                       _dot(p_b[s, h], v_all[rows_of[s], vc])], axis=0)
    row_parts = []
    for s in pairs:
        rows = rows_of[s]
        o = [intra[s, h] + lax.dot_general(q2[s, h], st_ref[h].astype(BF16), nt,
                                           preferred_element_type=F32) for h in heads]
        for h in heads:
            vc = slice(h * hv, (h + 1) * hv)
            st_ref[h] = dc_pair[s, h] * st_ref[h] + lax.dot_general(
                v_all[rows, vc], k2[s, h], tn, preferred_element_type=F32)
        parts = []
        for h in heads:
            vc = slice(h * hv, (h + 1) * hv)
            on = o[h] * lax.rsqrt(jnp.mean(o[h] * o[h], axis=-1, keepdims=True) + EPS)
            parts.append((on * gate_all[rows, vc]).astype(BF16))
        row_parts.append(jnp.concatenate(parts, axis=-1))
    a = jnp.concatenate(row_parts, axis=0)
    o_ref[...] = x + _dot(a, wout_ref[...])


def _gla_layer(x, seq, norm_w, w_in, w_gk_up, b_gk, out_norm_w, w_out, *, tm=256):
    t, d = x.shape
    rank, dk = w_gk_up.shape
    dv = out_norm_w.shape[0]
    tm = _tile(seq, tm)
    hk, hv = dk // GLA_HEADS, dv // GLA_HEADS
    n_main = 2 * dk + 2 * dv
    assert n_main % LANES == 0
    rank_pad = -(-rank // LANES) * LANES
    w_main = w_in.astype(BF16)
    w_low = jnp.pad(w_in[:, n_main:], ((0, 0), (0, rank_pad - rank))).astype(BF16)
    w_up = jnp.pad(w_gk_up, ((0, rank_pad - rank), (0, 0))).astype(BF16)
    return pl.pallas_call(
        functools.partial(_gla_layer_kernel, tiles_per_seq=seq // tm),
        grid=(t // tm,),
        in_specs=[pl.BlockSpec((tm, d), lambda i: (i, 0)),
                  _resident_spec((1, d)),
                  _resident_spec((d, n_main)),
                  _resident_spec((d, rank_pad)),
                  _resident_spec((rank_pad, dk)),
                  _resident_spec((1, dk)),
                  _resident_spec((1, dv)),
                  _resident_spec((dv, d))],
        out_specs=pl.BlockSpec((tm, d), lambda i: (i, 0)),
        out_shape=jax.ShapeDtypeStruct((t, d), F32),
        scratch_shapes=[pltpu.VMEM((GLA_HEADS, hv, hk), F32)],
        compiler_params=_params("arbitrary"),
        name="gla_layer",
    )(x, norm_w.reshape(1, d), w_main, w_low, w_up, b_gk.reshape(1, dk),
      out_norm_w.reshape(1, dv), w_out.astype(BF16))


def kernel(x, l0_norm1, l0_pool_w_in, l0_pool_w_group, l0_pool_scale, l0_pool_w_out, l0_norm2, l0_ffn_w_gate, l0_ffn_w_up, l0_ffn_w_down, l1_norm1, l1_sgu_w_in, l1_sgu_v_norm_w, l1_sgu_v_norm_b, l1_sgu_w_spatial, l1_sgu_b_spatial, l1_sgu_w_out, l1_norm2, l1_ffn_w_gate, l1_ffn_w_up, l1_ffn_w_down, l2_norm1, l2_gla_w_in, l2_gla_w_gk_up, l2_gla_b_gk, l2_gla_out_norm_w, l2_gla_w_out, l2_norm2, l2_ffn_w_gate, l2_ffn_w_up, l2_ffn_w_down, l3_norm1, l3_pool_w_in, l3_pool_w_group, l3_pool_scale, l3_pool_w_out, l3_norm2, l3_ffn_w_gate, l3_ffn_w_up, l3_ffn_w_down, final_norm_w):
    batch, seq, d = x.shape
    h = x.reshape(batch * seq, d)
    h, ffn0 = _pool_layer(h, seq, l0_norm1, l0_pool_w_in, l0_pool_w_group, l0_pool_scale,
                          l0_pool_w_out, (l0_ffn_w_gate, l0_ffn_w_up, l0_ffn_w_down))
    h, (sgu_w_in, sgu_w_out) = _ffn(h, l0_norm2, *ffn0,
                                    cast_weights=(l1_sgu_w_in, l1_sgu_w_out))
    h, ffn1 = _sgu_layer(h, l1_norm1, sgu_w_in, l1_sgu_v_norm_w, l1_sgu_v_norm_b,
                         l1_sgu_w_spatial, l1_sgu_b_spatial, sgu_w_out,
                         (l1_ffn_w_gate, l1_ffn_w_up, l1_ffn_w_down))
    h, (gla_w_out,) = _ffn(h, l1_norm2, *ffn1, cast_weights=(l2_gla_w_out,))
    h = _gla_layer(h, seq, l2_norm1, l2_gla_w_in, l2_gla_w_gk_up, l2_gla_b_gk,
                   l2_gla_out_norm_w, gla_w_out)
    group_shape = l3_pool_w_group.shape
    h, (pool_w_in, pool_w_group, pool_w_out) = _ffn(
        h, l2_norm2, l2_ffn_w_gate, l2_ffn_w_up, l2_ffn_w_down,
        cast_weights=(l3_pool_w_in, l3_pool_w_group.reshape(-1, group_shape[-1]),
                      l3_pool_w_out))
    h, ffn3 = _pool_layer(h, seq, l3_norm1, pool_w_in, pool_w_group.reshape(group_shape),
                          l3_pool_scale, pool_w_out,
                          (l3_ffn_w_gate, l3_ffn_w_up, l3_ffn_w_down))
    h, _ = _ffn(h, l3_norm2, *ffn3, final_norm_w)
    return h.reshape(batch, seq, d)
```

```python
import functools

import jax
import jax.numpy as jnp
from jax import lax
from jax.experimental import pallas as pl
from jax.experimental.pallas import tpu as pltpu

F32 = jnp.float32
BF16 = jnp.bfloat16

EPS = 1e-6
POOL_WINDOWS = (2, 4, 8, 16)
POOL_HALO = 24
SGU_CHUNK = 128
GLA_HEADS = 4
GLA_CHUNK = 64
GLA_GATE_TEMP = 16.0
LANES = 128
VMEM_LIMIT = 58 * 1024 * 1024


def _params(*semantics):
    return pltpu.CompilerParams(dimension_semantics=semantics,
                                vmem_limit_bytes=VMEM_LIMIT)


def _tile(n, want):
    t = min(n, want)
    assert n % t == 0, (n, want)
    return t


def _rms_norm(x, w):
    ms = jnp.mean(x * x, axis=-1, keepdims=True)
    return x * lax.rsqrt(ms + EPS) * w


def _dot(a, b):
    return jnp.dot(a, b, preferred_element_type=F32)


def _const_spec(shape):
    return pl.BlockSpec(shape, lambda *_: (0,) * len(shape))


def _resident_spec(shape):
    return pl.BlockSpec(shape, lambda *_: (0,) * len(shape),
                        pipeline_mode=pl.Buffered(1))


BF16_SUBLANES = 16


def _cast_plan(steps, weights):
    in_specs, out_specs, out_shapes = [], [], []
    for w in weights:
        rows, cols = w.shape
        rep = 1
        while (rows * rep) % (steps * BF16_SUBLANES):
            rep *= 2
            assert rep <= steps
        blk = rows * rep // steps

        def index_map(i, *_, rep=rep):
            return (i // rep, 0)

        in_specs.append(pl.BlockSpec((blk, cols), index_map))
        out_specs.append(pl.BlockSpec((blk, cols), index_map))
        out_shapes.append(jax.ShapeDtypeStruct((rows, cols), BF16))
    return in_specs, out_specs, out_shapes


def _cast_blocks(src_refs, dst_refs):
    for src, dst in zip(src_refs, dst_refs):
        dst[...] = src[...].astype(BF16)


def _pool_layer_kernel(x_ref, nw_ref, win_ref, wg_ref, sc_ref, wout_ref, c0, c1, c2,
                       o_ref, d0, d1, d2, zbuf, *, tiles_per_seq, sub):
    _cast_blocks((c0, c1, c2), (d0, d1, d2))
    tm = x_ref.shape[0]
    cg = wg_ref.shape[1]
    pos = pl.program_id(0) % tiles_per_seq
    halo = POOL_HALO

    @pl.when(pos == 0)
    def _():
        zbuf[0:halo, :] = jnp.zeros((halo, zbuf.shape[1]), F32)

    @pl.when(pos != 0)
    def _():
        zbuf[0:halo, :] = zbuf[tm:tm + halo, :]

    ext = sub + 16
    for r0 in range(0, tm, sub):
        h = _rms_norm(x_ref[r0:r0 + sub, :], nw_ref[...]).astype(BF16)
        zbuf[halo + r0:halo + r0 + sub, :] = _dot(h, win_ref[...])
    for r0 in range(0, tm, sub):
        x = x_ref[r0:r0 + sub, :]
        t_idx = pos * tm + r0 + lax.broadcasted_iota(jnp.int32, (sub, cg), 0)
        ys = []
        for g, w in enumerate(POOL_WINDOWS):
            cols = slice(g * cg, (g + 1) * cg)
            z = zbuf[halo + r0:halo + r0 + sub, cols]
            s = zbuf[8 + r0:8 + r0 + ext, cols]
            shift = 1
            while shift < w:
                s = s + pltpu.roll(s, shift, 0)
                shift *= 2
            count = jnp.minimum(t_idx + 1, w).astype(F32)
            p = s[16:, :] / count - z
            y = _dot(p.astype(BF16), wg_ref[g]) * sc_ref[:, cols]
            ys.append(y.astype(BF16))
        y = jnp.concatenate(ys, axis=-1)
        o_ref[r0:r0 + sub, :] = x + _dot(y, wout_ref[...])


def _pool_layer(x, seq, norm_w, w_in, w_group, scale, w_out, ffn_weights, *, tm=512, sub=256):
    t, d = x.shape
    p = w_in.shape[1]
    g, cg, _ = w_group.shape
    tm = _tile(seq, tm)
    sub = _tile(tm, sub)
    cast_in, cast_out, cast_shapes = _cast_plan(t // tm, ffn_weights)
    out, *ffn_bf16 = pl.pallas_call(
        functools.partial(_pool_layer_kernel, tiles_per_seq=seq // tm, sub=sub),
        grid=(t // tm,),
        in_specs=[pl.BlockSpec((tm, d), lambda i: (i, 0)),
                  _resident_spec((1, d)),
                  _resident_spec((d, p)),
                  _resident_spec((g, cg, cg)),
                  _resident_spec((1, p)),
                  _resident_spec((p, d))] + cast_in,
        out_specs=[pl.BlockSpec((tm, d), lambda i: (i, 0))] + cast_out,
        out_shape=[jax.ShapeDtypeStruct((t, d), F32)] + cast_shapes,
        scratch_shapes=[pltpu.VMEM((tm + POOL_HALO, p), F32)],
        compiler_params=_params("arbitrary"),
        name="pool_layer",
    )(x, norm_w.reshape(1, d), w_in.astype(BF16), w_group.astype(BF16),
      scale.reshape(1, p), w_out.astype(BF16), *ffn_weights)
    return out, ffn_bf16


FFN_NORM_STEPS = 4


def _ffn_kernel(x_hbm, nw_ref, wg_ref, wu_ref, wd_ref, *rest, n_cast, final_norm):
    rest = list(rest)
    fnw_ref = rest.pop(0) if final_norm else None
    cast_src = [rest.pop(0) for _ in range(n_cast)]
    o_hbm = rest.pop(0)
    cast_dst = [rest.pop(0) for _ in range(n_cast)]
    acc0, acc1, hbuf0, hbuf1, in_sem, out_sem = rest
    acc, hbuf = (acc0, acc1), (hbuf0, hbuf1)
    i, j = pl.program_id(0), pl.program_id(1)
    nt, nj = pl.num_programs(0), pl.num_programs(1)
    tm = acc0.shape[0]
    rq = tm // FFN_NORM_STEPS
    q = j - (nj - FFN_NORM_STEPS)
    has_next = i + 1 < nt

    def x_copy(tile, s):
        return pltpu.make_async_copy(x_hbm.at[pl.ds(tile * tm, tm), :], acc[s], in_sem.at[s])

    def o_copy(tile, s):
        return pltpu.make_async_copy(acc[s], o_hbm.at[pl.ds(tile * tm, tm), :], out_sem.at[s])

    if n_cast:
        @pl.when(j == 0)
        def _():
            _cast_blocks(cast_src, cast_dst)

    @pl.when((i == 0) & (j == 0))
    def _():
        first = x_copy(0, 0)
        first.start()
        first.wait()
        hbuf[0][...] = _rms_norm(acc[0][...], nw_ref[...]).astype(BF16)

    def step(slot, with_norm):
        other = 1 - slot
        h = hbuf[slot][...]
        tfh = wg_ref.shape[1] // 2
        rp = rq // 4

        def norm_piece(p):
            if with_norm:
                rows = pl.ds(pl.multiple_of(q * rq + p * rp, rp), rp)
                hbuf[other][rows, :] = _rms_norm(acc[other][rows, :], nw_ref[...]).astype(BF16)

        parts = []
        for half in range(2):
            cols = slice(half * tfh, (half + 1) * tfh)
            g = _dot(h, wg_ref[:, cols].astype(BF16))
            norm_piece(2 * half)
            u = _dot(h, wu_ref[:, cols].astype(BF16))
            norm_piece(2 * half + 1)
            parts.append((g * jax.nn.sigmoid(g) * u).astype(BF16))
        a = jnp.concatenate(parts, axis=-1)
        acc[slot][...] += _dot(a, wd_ref[...].astype(BF16))

    for slot in (0, 1):
        other = 1 - slot
        mine = (i % 2) == slot

        @pl.when(mine & (j == 1) & (i >= 1))
        def _():
            o_copy(i - 1, other).wait()

        @pl.when(mine & (j == 1) & has_next)
        def _():
            x_copy(i + 1, other).start()

        @pl.when(mine & (q == 0) & has_next)
        def _():
            x_copy(i + 1, other).wait()

        norm_next = (q >= 0) & has_next
        pl.when(mine & norm_next)(functools.partial(step, slot, True))
        pl.when(mine & jnp.logical_not(norm_next))(functools.partial(step, slot, False))

        @pl.when(mine & (j == nj - 1))
        def _():
            if fnw_ref is not None:
                acc[slot][...] = _rms_norm(acc[slot][...], fnw_ref[...])
            o_copy(i, slot).start()

        @pl.when(mine & (j == nj - 1) & (i == nt - 1))
        def _():
            o_copy(i, slot).wait()


def _ffn(x, norm_w, w_gate, w_up, w_down, final_norm_w=None, cast_weights=(),
         *, tm=1024, tf=512):
    t, d = x.shape
    f = w_gate.shape[1]
    tm, tf = _tile(t, tm), _tile(f, tf)
    nt, nj = t // tm, f // tf
    assert nt >= 2 and nj >= FFN_NORM_STEPS + 2 and tm % FFN_NORM_STEPS == 0
    in_specs = [pl.BlockSpec(memory_space=pl.ANY),
                _const_spec((1, d)),
                pl.BlockSpec((d, tf), lambda i, j: (0, j)),
                pl.BlockSpec((d, tf), lambda i, j: (0, j)),
                pl.BlockSpec((tf, d), lambda i, j: (j, 0))]
    args = [x, norm_w.reshape(1, d), w_gate, w_up, w_down]
    if final_norm_w is not None:
        in_specs.append(_const_spec((1, d)))
        args.append(final_norm_w.reshape(1, d))
    cast_in, cast_out, cast_shapes = _cast_plan(nt, cast_weights)
    out, *cast_bf16 = pl.pallas_call(
        functools.partial(_ffn_kernel, n_cast=len(cast_weights),
                          final_norm=final_norm_w is not None),
        grid=(nt, nj),
        in_specs=in_specs + cast_in,
        out_specs=[pl.BlockSpec(memory_space=pl.ANY)] + cast_out,
        out_shape=[jax.ShapeDtypeStruct((t, d), F32)] + cast_shapes,
        scratch_shapes=[pltpu.VMEM((tm, d), F32), pltpu.VMEM((tm, d), F32),
                        pltpu.VMEM((tm, d), BF16), pltpu.VMEM((tm, d), BF16),
                        pltpu.SemaphoreType.DMA((2,)),
                        pltpu.SemaphoreType.DMA((2,))],
        compiler_params=_params("arbitrary", "arbitrary"),
        name="ffn",
    )(*args, *cast_weights)
    return out, cast_bf16


def _gelu(z):
    return 0.5 * z * (1.0 + lax.erf(z * (2.0 ** -0.5)))


def _sgu_layer_kernel(x_ref, nw_ref, win_ref, lnw_ref, lnb_ref, ws_ref, bs_ref, wout_ref,
                      c0, c1, c2, o_ref, d0, d1, d2, *, sub):
    _cast_blocks((c0, c1, c2), (d0, d1, d2))
    tm = x_ref.shape[0]
    dim = lnw_ref.shape[1]
    heads = ws_ref.shape[0]
    hd = dim // heads
    c = SGU_CHUNK

    row = lax.broadcasted_iota(jnp.int32, (c, c), 0)
    col = lax.broadcasted_iota(jnp.int32, (c, c), 1)
    causal = col <= row
    wms = [jnp.where(causal, ws_ref[h], 0.0).astype(BF16) for h in range(heads)]

    for r0 in range(0, tm, sub):
        x = x_ref[r0:r0 + sub, :]
        h = _rms_norm(x, nw_ref[...]).astype(BF16)
        v = _gelu(_dot(h, win_ref[:, dim:]))
        mu = jnp.mean(v, axis=-1, keepdims=True)
        vc = v - mu
        var = jnp.mean(vc * vc, axis=-1, keepdims=True)
        vn = (vc * lax.rsqrt(var + EPS) * lnw_ref[...] + lnb_ref[...]).astype(BF16)
        parts = []
        u_all = _gelu(_dot(h, win_ref[:, :dim]))
        for hh in range(heads):
            cols = slice(hh * hd, (hh + 1) * hd)
            bias = bs_ref[:, hh:hh + 1]
            gate = jnp.concatenate(
                [_dot(wms[hh], vn[n * c:(n + 1) * c, cols]) + bias for n in range(sub // c)],
                axis=0)
            parts.append((u_all[:, cols] * gate).astype(BF16))
        a = jnp.concatenate(parts, axis=-1)
        o_ref[r0:r0 + sub, :] = x + _dot(a, wout_ref[...])


def _sgu_layer(x, norm_w, w_in, ln_w, ln_b, w_spatial, b_spatial, w_out, ffn_weights,
               *, tm=512, sub=256):
    t, d = x.shape
    dim = w_out.shape[0]
    heads, c, _ = w_spatial.shape
    tm = _tile(t, tm)
    sub = _tile(tm, sub)
    cast_in, cast_out, cast_shapes = _cast_plan(t // tm, ffn_weights)
    out, *ffn_bf16 = pl.pallas_call(
        functools.partial(_sgu_layer_kernel, sub=sub),
        grid=(t // tm,),
        in_specs=[pl.BlockSpec((tm, d), lambda i: (i, 0)),
                  _resident_spec((1, d)),
                  _resident_spec((d, 2 * dim)),
                  _resident_spec((1, dim)),
                  _resident_spec((1, dim)),
                  _resident_spec((heads, c, c)),
                  _resident_spec((c, heads)),
                  _resident_spec((dim, d))] + cast_in,
        out_specs=[pl.BlockSpec((tm, d), lambda i: (i, 0))] + cast_out,
        out_shape=[jax.ShapeDtypeStruct((t, d), F32)] + cast_shapes,
        compiler_params=_params("arbitrary"),
        name="sgu_layer",
    )(x, norm_w.reshape(1, d), w_in.astype(BF16), ln_w.reshape(1, dim), ln_b.reshape(1, dim),
      w_spatial, b_spatial.T, w_out.astype(BF16), *ffn_weights)
    return out, ffn_bf16


def _gla_layer_kernel(x_ref, nw_ref, w_ref, wlo_ref, wup_ref, bgk_ref, onw_ref, wout_ref,
                      o_ref, st_ref, *, tiles_per_seq):
    tm = x_ref.shape[0]
    dkt = wup_ref.shape[1]
    dvt = onw_ref.shape[1]
    hk, hv = dkt // GLA_HEADS, dvt // GLA_HEADS
    c = GLA_CHUNK
    heads = range(GLA_HEADS)

    @pl.when(pl.program_id(0) % tiles_per_seq == 0)
    def _():
        st_ref[...] = jnp.zeros(st_ref.shape, F32)

    row_k = lax.broadcasted_iota(jnp.int32, (2 * c, hk), 0)
    in_a = row_k < c
    row_in_chunk = jnp.where(in_a, row_k, row_k - c)
    row = lax.broadcasted_iota(jnp.int32, (c, c), 0)
    col = lax.broadcasted_iota(jnp.int32, (c, c), 1)
    causal = col <= row
    nt = (((1,), (1,)), ((), ()))
    tn = (((0,), (0,)), ((), ()))

    x = x_ref[...]
    hn = _rms_norm(x, nw_ref[...]).astype(BF16)
    low = _dot(hn, wlo_ref[...]).astype(BF16)
    q_all = _dot(hn, w_ref[:, 0:dkt]) * (hk ** -0.5)
    k_all = _dot(hn, w_ref[:, dkt:2 * dkt])
    g_all = jax.nn.log_sigmoid(_dot(low, wup_ref[...]) + bgk_ref[...]) / GLA_GATE_TEMP
    v_all = _dot(hn, w_ref[:, 2 * dkt:2 * dkt + dvt]).astype(BF16)
    r_all = _dot(hn, w_ref[:, 2 * dkt + dvt:])
    gate_all = r_all * jax.nn.sigmoid(r_all) * onw_ref[...]

    pairs = range(tm // (2 * c))
    rows_of = [slice(s * 2 * c, (s + 1) * 2 * c) for s in pairs]
    units = [(s, h) for s in pairs for h in heads]
    q_in, k_in, k_dec_a, q2, k2, dc_pair = {}, {}, {}, {}, {}, {}
    for s, h in units:
        rows = rows_of[s]
        kc = slice(h * hk, (h + 1) * hk)
        b = g_all[rows, kc]
        shift = 1
        while shift < c:
            b = b + jnp.where(row_in_chunk >= shift, pltpu.roll(b, shift, 0), 0.0)
            shift *= 2
        b_last_a, b_last_b = b[c - 1:c, :], b[2 * c - 1:2 * c, :]
        dc_a, dc_b = jnp.exp(b_last_a), jnp.exp(b_last_b)
        k = k_all[rows, kc]
        qi = q_all[rows, kc] * jnp.exp(b)
        kd = k * jnp.exp(jnp.where(in_a, b_last_a, b_last_b) - b)
        q_in[s, h] = qi.astype(BF16)
        k_in[s, h] = (k * jnp.exp(-b)).astype(BF16)
        k_dec_a[s, h] = kd[:c].astype(BF16)
        q2[s, h] = jnp.where(in_a, qi, qi * dc_a).astype(BF16)
        k2[s, h] = jnp.where(in_a, kd * dc_b, kd).astype(BF16)
        dc_pair[s, h] = dc_a * dc_b
    s_aa = {u: lax.dot_general(q_in[u][:c], k_in[u][:c], nt, preferred_element_type=F32)
            for u in units}
    s_bb = {u: lax.dot_general(q_in[u][c:], k_in[u][c:], nt, preferred_element_type=F32)
            for u in units}
    s_ba = {u: lax.dot_general(q_in[u][c:], k_dec_a[u], nt, preferred_element_type=F32)
            for u in units}
    p_a = {u: jnp.where(causal, s_aa[u], 0.0).astype(BF16) for u in units}
    p_b = {u: jnp.concatenate([s_ba[u], jnp.where(causal, s_bb[u], 0.0)], axis=1).astype(BF16)
           for u in units}
    intra = {}
    for s, h in units:
        vc = slice(h * hv, (h + 1) * hv)
        rows_a = slice(s * 2 * c, s * 2 * c + c)
        intra[s, h] = jnp.concatenate([_dot(p_a[s, h], v_all[rows_a, vc]),
                                       _dot(p_b[s, h], v_all[rows_of[s], vc])], axis=0)
    row_parts = []
    for s in pairs:
        rows = rows_of[s]
        o = [intra[s, h] + lax.dot_general(q2[s, h], st_ref[h].astype(BF16), nt,
                                           preferred_element_type=F32) for h in heads]
        for h in heads:
            vc = slice(h * hv, (h + 1) * hv)
            st_ref[h] = dc_pair[s, h] * st_ref[h] + lax.dot_general(
                v_all[rows, vc], k2[s, h], tn, preferred_element_type=F32)
        parts = []
        for h in heads:
            vc = slice(h * hv, (h + 1) * hv)
            on = o[h] * lax.rsqrt(jnp.mean(o[h] * o[h], axis=-1, keepdims=True) + EPS)
            parts.append((on * gate_all[rows, vc]).astype(BF16))
        row_parts.append(jnp.concatenate(parts, axis=-1))
    a = jnp.concatenate(row_parts, axis=0)
    o_ref[...] = x + _dot(a, wout_ref[...])


def _gla_layer(x, seq, norm_w, w_in, w_gk_up, b_gk, out_norm_w, w_out, *, tm=256):
    t, d = x.shape
    rank, dk = w_gk_up.shape
    dv = out_norm_w.shape[0]
    tm = _tile(seq, tm)
    hk, hv = dk // GLA_HEADS, dv // GLA_HEADS
    n_main = 2 * dk + 2 * dv
    assert n_main % LANES == 0
    rank_pad = -(-rank // LANES) * LANES
    w_main = w_in.astype(BF16)
    w_low = jnp.pad(w_in[:, n_main:], ((0, 0), (0, rank_pad - rank))).astype(BF16)
    w_up = jnp.pad(w_gk_up, ((0, rank_pad - rank), (0, 0))).astype(BF16)
    return pl.pallas_call(
        functools.partial(_gla_layer_kernel, tiles_per_seq=seq // tm),
        grid=(t // tm,),
        in_specs=[pl.BlockSpec((tm, d), lambda i: (i, 0)),
                  _resident_spec((1, d)),
                  _resident_spec((d, n_main)),
                  _resident_spec((d, rank_pad)),
                  _resident_spec((rank_pad, dk)),
                  _resident_spec((1, dk)),
                  _resident_spec((1, dv)),
                  _resident_spec((dv, d))],
        out_specs=pl.BlockSpec((tm, d), lambda i: (i, 0)),
        out_shape=jax.ShapeDtypeStruct((t, d), F32),
        scratch_shapes=[pltpu.VMEM((GLA_HEADS, hv, hk), F32)],
        compiler_params=_params("arbitrary"),
        name="gla_layer",
    )(x, norm_w.reshape(1, d), w_main, w_low, w_up, b_gk.reshape(1, dk),
      out_norm_w.reshape(1, dv), w_out.astype(BF16))


def kernel(x, l0_norm1, l0_pool_w_in, l0_pool_w_group, l0_pool_scale, l0_pool_w_out, l0_norm2, l0_ffn_w_gate, l0_ffn_w_up, l0_ffn_w_down, l1_norm1, l1_sgu_w_in, l1_sgu_v_norm_w, l1_sgu_v_norm_b, l1_sgu_w_spatial, l1_sgu_b_spatial, l1_sgu_w_out, l1_norm2, l1_ffn_w_gate, l1_ffn_w_up, l1_ffn_w_down, l2_norm1, l2_gla_w_in, l2_gla_w_gk_up, l2_gla_b_gk, l2_gla_out_norm_w, l2_gla_w_out, l2_norm2, l2_ffn_w_gate, l2_ffn_w_up, l2_ffn_w_down, l3_norm1, l3_pool_w_in, l3_pool_w_group, l3_pool_scale, l3_pool_w_out, l3_norm2, l3_ffn_w_gate, l3_ffn_w_up, l3_ffn_w_down, final_norm_w):
    batch, seq, d = x.shape
    h = x.reshape(batch * seq, d)
    h, ffn0 = _pool_layer(h, seq, l0_norm1, l0_pool_w_in, l0_pool_w_group, l0_pool_scale,
                          l0_pool_w_out, (l0_ffn_w_gate, l0_ffn_w_up, l0_ffn_w_down))
    h, (sgu_w_in, sgu_w_out) = _ffn(h, l0_norm2, *ffn0,
                                    cast_weights=(l1_sgu_w_in, l1_sgu_w_out))
    h, ffn1 = _sgu_layer(h, l1_norm1, sgu_w_in, l1_sgu_v_norm_w, l1_sgu_v_norm_b,
                         l1_sgu_w_spatial, l1_sgu_b_spatial, sgu_w_out,
                         (l1_ffn_w_gate, l1_ffn_w_up, l1_ffn_w_down))
    h, (gla_w_out,) = _ffn(h, l1_norm2, *ffn1, cast_weights=(l2_gla_w_out,))
    h = _gla_layer(h, seq, l2_norm1, l2_gla_w_in, l2_gla_w_gk_up, l2_gla_b_gk,
                   l2_gla_out_norm_w, gla_w_out)
    group_shape = l3_pool_w_group.shape
    h, (pool_w_in, pool_w_group, pool_w_out) = _ffn(
        h, l2_norm2, l2_ffn_w_gate, l2_ffn_w_up, l2_ffn_w_down,
        cast_weights=(l3_pool_w_in, l3_pool_w_group.reshape(-1, group_shape[-1]),
                      l3_pool_w_out))
    h, ffn3 = _pool_layer(h, seq, l3_norm1, pool_w_in, pool_w_group.reshape(group_shape),
                          l3_pool_scale, pool_w_out,
                          (l3_ffn_w_gate, l3_ffn_w_up, l3_ffn_w_down))
    h, _ = _ffn(h, l3_norm2, *ffn3, final_norm_w)
    return h.reshape(batch, seq, d)
```

```python
import functools

import jax
import jax.numpy as jnp
from jax import lax
from jax.experimental import pallas as pl
from jax.experimental.pallas import tpu as pltpu

F32 = jnp.float32
BF16 = jnp.bfloat16

EPS = 1e-6
POOL_WINDOWS = (2, 4, 8, 16)
POOL_HALO = 24
SGU_CHUNK = 128
GLA_HEADS = 4
GLA_CHUNK = 64
GLA_GATE_TEMP = 16.0
LANES = 128
VMEM_LIMIT = 58 * 1024 * 1024


def _params(*semantics):
    return pltpu.CompilerParams(dimension_semantics=semantics,
                                vmem_limit_bytes=VMEM_LIMIT)


def _tile(n, want):
    t = min(n, want)
    assert n % t == 0, (n, want)
    return t


def _rms_norm(x, w):
    ms = jnp.mean(x * x, axis=-1, keepdims=True)
    return x * lax.rsqrt(ms + EPS) * w


def _dot(a, b):
    return jnp.dot(a, b, preferred_element_type=F32)


def _const_spec(shape):
    return pl.BlockSpec(shape, lambda *_: (0,) * len(shape))


def _resident_spec(shape):
    return pl.BlockSpec(shape, lambda *_: (0,) * len(shape),
                        pipeline_mode=pl.Buffered(1))


BF16_SUBLANES = 16


def _cast_plan(steps, weights):
    in_specs, out_specs, out_shapes = [], [], []
    for w in weights:
        rows, cols = w.shape
        rep = 1
        while (rows * rep) % (steps * BF16_SUBLANES):
            rep *= 2
            assert rep <= steps
        blk = rows * rep // steps

        def index_map(i, *_, rep=rep):
            return (i // rep, 0)

        in_specs.append(pl.BlockSpec((blk, cols), index_map))
        out_specs.append(pl.BlockSpec((blk, cols), index_map))
        out_shapes.append(jax.ShapeDtypeStruct((rows, cols), BF16))
    return in_specs, out_specs, out_shapes


def _cast_blocks(src_refs, dst_refs):
    for src, dst in zip(src_refs, dst_refs):
        dst[...] = src[...].astype(BF16)


def _pool_layer_kernel(x_ref, nw_ref, win_ref, wg_ref, sc_ref, wout_ref, c0, c1, c2,
                       o_ref, d0, d1, d2, zbuf, *, tiles_per_seq, sub):
    _cast_blocks((c0, c1, c2), (d0, d1, d2))
    tm = x_ref.shape[0]
    cg = wg_ref.shape[1]
    pos = pl.program_id(0) % tiles_per_seq
    halo = POOL_HALO

    @pl.when(pos == 0)
    def _():
        zbuf[0:halo, :] = jnp.zeros((halo, zbuf.shape[1]), F32)

    @pl.when(pos != 0)
    def _():
        zbuf[0:halo, :] = zbuf[tm:tm + halo, :]

    ext = sub + 16
    for r0 in range(0, tm, sub):
        h = _rms_norm(x_ref[r0:r0 + sub, :], nw_ref[...]).astype(BF16)
        zbuf[halo + r0:halo + r0 + sub, :] = _dot(h, win_ref[...])
    for r0 in range(0, tm, sub):
        x = x_ref[r0:r0 + sub, :]
        t_idx = pos * tm + r0 + lax.broadcasted_iota(jnp.int32, (sub, cg), 0)
        ys = []
        for g, w in enumerate(POOL_WINDOWS):
            cols = slice(g * cg, (g + 1) * cg)
            z = zbuf[halo + r0:halo + r0 + sub, cols]
            s = zbuf[8 + r0:8 + r0 + ext, cols]
            shift = 1
            while shift < w:
                s = s + pltpu.roll(s, shift, 0)
                shift *= 2
            count = jnp.minimum(t_idx + 1, w).astype(F32)
            p = s[16:, :] / count - z
            y = _dot(p.astype(BF16), wg_ref[g]) * sc_ref[:, cols]
            ys.append(y.astype(BF16))
        y = jnp.concatenate(ys, axis=-1)
        o_ref[r0:r0 + sub, :] = x + _dot(y, wout_ref[...])


def _pool_layer(x, seq, norm_w, w_in, w_group, scale, w_out, ffn_weights, *, tm=512, sub=256):
    t, d = x.shape
    p = w_in.shape[1]
    g, cg, _ = w_group.shape
    tm = _tile(seq, tm)
    sub = _tile(tm, sub)
    cast_in, cast_out, cast_shapes = _cast_plan(t // tm, ffn_weights)
    out, *ffn_bf16 = pl.pallas_call(
        functools.partial(_pool_layer_kernel, tiles_per_seq=seq // tm, sub=sub),
        grid=(t // tm,),
        in_specs=[pl.BlockSpec((tm, d), lambda i: (i, 0)),
                  _resident_spec((1, d)),
                  _resident_spec((d, p)),
                  _resident_spec((g, cg, cg)),
                  _resident_spec((1, p)),
                  _resident_spec((p, d))] + cast_in,
        out_specs=[pl.BlockSpec((tm, d), lambda i: (i, 0))] + cast_out,
        out_shape=[jax.ShapeDtypeStruct((t, d), F32)] + cast_shapes,
        scratch_shapes=[pltpu.VMEM((tm + POOL_HALO, p), F32)],
        compiler_params=_params("arbitrary"),
        name="pool_layer",
    )(x, norm_w.reshape(1, d), w_in.astype(BF16), w_group.astype(BF16),
      scale.reshape(1, p), w_out.astype(BF16), *ffn_weights)
    return out, ffn_bf16


FFN_NORM_STEPS = 4


def _ffn_kernel(x_hbm, nw_ref, wg_ref, wu_ref, wd_ref, *rest, n_cast, final_norm):
    rest = list(rest)
    fnw_ref = rest.pop(0) if final_norm else None
    cast_src = [rest.pop(0) for _ in range(n_cast)]
    o_hbm = rest.pop(0)
    cast_dst = [rest.pop(0) for _ in range(n_cast)]
    acc0, acc1, hbuf0, hbuf1, in_sem, out_sem = rest
    acc, hbuf = (acc0, acc1), (hbuf0, hbuf1)
    i, j = pl.program_id(0), pl.program_id(1)
    nt, nj = pl.num_programs(0), pl.num_programs(1)
    tm = acc0.shape[0]
    rq = tm // FFN_NORM_STEPS
    q = j - (nj - FFN_NORM_STEPS)
    has_next = i + 1 < nt

    def x_copy(tile, s):
        return pltpu.make_async_copy(x_hbm.at[pl.ds(tile * tm, tm), :], acc[s], in_sem.at[s])

    def o_copy(tile, s):
        return pltpu.make_async_copy(acc[s], o_hbm.at[pl.ds(tile * tm, tm), :], out_sem.at[s])

    if n_cast:
        @pl.when(j == 0)
        def _():
            _cast_blocks(cast_src, cast_dst)

    @pl.when((i == 0) & (j == 0))
    def _():
        first = x_copy(0, 0)
        first.start()
        first.wait()
        hbuf[0][...] = _rms_norm(acc[0][...], nw_ref[...]).astype(BF16)

    def step(slot, with_norm):
        other = 1 - slot
        h = hbuf[slot][...]
        tfh = wg_ref.shape[1] // 2
        rp = rq // 4

        def norm_piece(p):
            if with_norm:
                rows = pl.ds(pl.multiple_of(q * rq + p * rp, rp), rp)
                hbuf[other][rows, :] = _rms_norm(acc[other][rows, :], nw_ref[...]).astype(BF16)

        parts = []
        for half in range(2):
            cols = slice(half * tfh, (half + 1) * tfh)
            g = _dot(h, wg_ref[:, cols].astype(BF16))
            norm_piece(2 * half)
            u = _dot(h, wu_ref[:, cols].astype(BF16))
            norm_piece(2 * half + 1)
            parts.append((g * jax.nn.sigmoid(g) * u).astype(BF16))
        a = jnp.concatenate(parts, axis=-1)
        acc[slot][...] += _dot(a, wd_ref[...].astype(BF16))

    for slot in (0, 1):
        other = 1 - slot
        mine = (i % 2) == slot

        @pl.when(mine & (j == 1) & (i >= 1))
        def _():
            o_copy(i - 1, other).wait()

        @pl.when(mine & (j == 1) & has_next)
        def _():
            x_copy(i + 1, other).start()

        @pl.when(mine & (q == 0) & has_next)
        def _():
            x_copy(i + 1, other).wait()

        norm_next = (q >= 0) & has_next
        pl.when(mine & norm_next)(functools.partial(step, slot, True))
        pl.when(mine & jnp.logical_not(norm_next))(functools.partial(step, slot, False))

        @pl.when(mine & (j == nj - 1))
        def _():
            if fnw_ref is not None:
                acc[slot][...] = _rms_norm(acc[slot][...], fnw_ref[...])
            o_copy(i, slot).start()

        @pl.when(mine & (j == nj - 1) & (i == nt - 1))
        def _():
            o_copy(i, slot).wait()


def _ffn(x, norm_w, w_gate, w_up, w_down, final_norm_w=None, cast_weights=(),
         *, tm=1024, tf=512):
    t, d = x.shape
    f = w_gate.shape[1]
    tm, tf = _tile(t, tm), _tile(f, tf)
    nt, nj = t // tm, f // tf
    assert nt >= 2 and nj >= FFN_NORM_STEPS + 2 and tm % FFN_NORM_STEPS == 0
    in_specs = [pl.BlockSpec(memory_space=pl.ANY),
                _const_spec((1, d)),
                pl.BlockSpec((d, tf), lambda i, j: (0, j)),
                pl.BlockSpec((d, tf), lambda i, j: (0, j)),
                pl.BlockSpec((tf, d), lambda i, j: (j, 0))]
    args = [x, norm_w.reshape(1, d), w_gate, w_up, w_down]
    if final_norm_w is not None:
        in_specs.append(_const_spec((1, d)))
        args.append(final_norm_w.reshape(1, d))
    cast_in, cast_out, cast_shapes = _cast_plan(nt, cast_weights)
    out, *cast_bf16 = pl.pallas_call(
        functools.partial(_ffn_kernel, n_cast=len(cast_weights),
                          final_norm=final_norm_w is not None),
        grid=(nt, nj),
        in_specs=in_specs + cast_in,
        out_specs=[pl.BlockSpec(memory_space=pl.ANY)] + cast_out,
        out_shape=[jax.ShapeDtypeStruct((t, d), F32)] + cast_shapes,
        scratch_shapes=[pltpu.VMEM((tm, d), F32), pltpu.VMEM((tm, d), F32),
                        pltpu.VMEM((tm, d), BF16), pltpu.VMEM((tm, d), BF16),
                        pltpu.SemaphoreType.DMA((2,)),
                        pltpu.SemaphoreType.DMA((2,))],
        compiler_params=_params("arbitrary", "arbitrary"),
        name="ffn",
    )(*args, *cast_weights)
    return out, cast_bf16


def _gelu(z):
    return 0.5 * z * (1.0 + lax.erf(z * (2.0 ** -0.5)))


def _sgu_layer_kernel(x_ref, nw_ref, win_ref, lnw_ref, lnb_ref, ws_ref, bs_ref, wout_ref,
                      c0, c1, c2, o_ref, d0, d1, d2, *, sub):
    _cast_blocks((c0, c1, c2), (d0, d1, d2))
    tm = x_ref.shape[0]
    dim = lnw_ref.shape[1]
    heads = ws_ref.shape[0]
    hd = dim // heads
    c = SGU_CHUNK

    row = lax.broadcasted_iota(jnp.int32, (c, c), 0)
    col = lax.broadcasted_iota(jnp.int32, (c, c), 1)
    causal = col <= row
    wms = [jnp.where(causal, ws_ref[h], 0.0).astype(BF16) for h in range(heads)]

    for r0 in range(0, tm, sub):
        x = x_ref[r0:r0 + sub, :]
        h = _rms_norm(x, nw_ref[...]).astype(BF16)
        v = _gelu(_dot(h, win_ref[:, dim:]))
        mu = jnp.mean(v, axis=-1, keepdims=True)
        vc = v - mu
        var = jnp.mean(vc * vc, axis=-1, keepdims=True)
        vn = (vc * lax.rsqrt(var + EPS) * lnw_ref[...] + lnb_ref[...]).astype(BF16)
        parts = []
        u_all = _gelu(_dot(h, win_ref[:, :dim]))
        for hh in range(heads):
            cols = slice(hh * hd, (hh + 1) * hd)
            bias = bs_ref[:, hh:hh + 1]
            gate = jnp.concatenate(
                [_dot(wms[hh], vn[n * c:(n + 1) * c, cols]) + bias for n in range(sub // c)],
                axis=0)
            parts.append((u_all[:, cols] * gate).astype(BF16))
        a = jnp.concatenate(parts, axis=-1)
        o_ref[r0:r0 + sub, :] = x + _dot(a, wout_ref[...])


def _sgu_layer(x, norm_w, w_in, ln_w, ln_b, w_spatial, b_spatial, w_out, ffn_weights,
               *, tm=512, sub=256):
    t, d = x.shape
    dim = w_out.shape[0]
    heads, c, _ = w_spatial.shape
    tm = _tile(t, tm)
    sub = _tile(tm, sub)
    cast_in, cast_out, cast_shapes = _cast_plan(t // tm, ffn_weights)
    out, *ffn_bf16 = pl.pallas_call(
        functools.partial(_sgu_layer_kernel, sub=sub),
        grid=(t // tm,),
        in_specs=[pl.BlockSpec((tm, d), lambda i: (i, 0)),
                  _resident_spec((1, d)),
                  _resident_spec((d, 2 * dim)),
                  _resident_spec((1, dim)),
                  _resident_spec((1, dim)),
                  _resident_spec((heads, c, c)),
                  _resident_spec((c, heads)),
                  _resident_spec((dim, d))] + cast_in,
        out_specs=[pl.BlockSpec((tm, d), lambda i: (i, 0))] + cast_out,
        out_shape=[jax.ShapeDtypeStruct((t, d), F32)] + cast_shapes,
        compiler_params=_params("arbitrary"),
        name="sgu_layer",
    )(x, norm_w.reshape(1, d), w_in.astype(BF16), ln_w.reshape(1, dim), ln_b.reshape(1, dim),
      w_spatial, b_spatial.T, w_out.astype(BF16), *ffn_weights)
    return out, ffn_bf16


def _gla_layer_kernel(x_ref, nw_ref, w_ref, wlo_ref, wup_ref, bgk_ref, onw_ref, wout_ref,
                      o_ref, st_ref, *, tiles_per_seq):
    tm = x_ref.shape[0]
    dkt = wup_ref.shape[1]
    dvt = onw_ref.shape[1]
    hk, hv = dkt // GLA_HEADS, dvt // GLA_HEADS
    c = GLA_CHUNK
    heads = range(GLA_HEADS)

    @pl.when(pl.program_id(0) % tiles_per_seq == 0)
    def _():
        st_ref[...] = jnp.zeros(st_ref.shape, F32)

    row_k = lax.broadcasted_iota(jnp.int32, (2 * c, hk), 0)
    in_a = row_k < c
    row_in_chunk = jnp.where(in_a, row_k, row_k - c)
    row = lax.broadcasted_iota(jnp.int32, (c, c), 0)
    col = lax.broadcasted_iota(jnp.int32, (c, c), 1)
    causal = col <= row
    nt = (((1,), (1,)), ((), ()))
    tn = (((0,), (0,)), ((), ()))

    x = x_ref[...]
    hn = _rms_norm(x, nw_ref[...]).astype(BF16)
    low = _dot(hn, wlo_ref[...]).astype(BF16)
    q_all = _dot(hn, w_ref[:, 0:dkt]) * (hk ** -0.5)
    k_all = _dot(hn, w_ref[:, dkt:2 * dkt])
    g_all = jax.nn.log_sigmoid(_dot(low, wup_ref[...]) + bgk_ref[...]) / GLA_GATE_TEMP
    v_all = _dot(hn, w_ref[:, 2 * dkt:2 * dkt + dvt]).astype(BF16)
    r_all = _dot(hn, w_ref[:, 2 * dkt + dvt:])
    gate_all = r_all * jax.nn.sigmoid(r_all) * onw_ref[...]

    pairs = range(tm // (2 * c))
    rows_of = [slice(s * 2 * c, (s + 1) * 2 * c) for s in pairs]
    units = [(s, h) for s in pairs for h in heads]
    q_in, k_in, k_dec_a, q2, k2, dc_pair = {}, {}, {}, {}, {}, {}
    for s, h in units:
        rows = rows_of[s]
        kc = slice(h * hk, (h + 1) * hk)
        b = g_all[rows, kc]
        shift = 1
        while shift < c:
            b = b + jnp.where(row_in_chunk >= shift, pltpu.roll(b, shift, 0), 0.0)
            shift *= 2
        b_last_a, b_last_b = b[c - 1:c, :], b[2 * c - 1:2 * c, :]
        dc_a, dc_b = jnp.exp(b_last_a), jnp.exp(b_last_b)
        k = k_all[rows, kc]
        qi = q_all[rows, kc] * jnp.exp(b)
        kd = k * jnp.exp(jnp.where(in_a, b_last_a, b_last_b) - b)
        q_in[s, h] = qi.astype(BF16)
        k_in[s, h] = (k * jnp.exp(-b)).astype(BF16)
        k_dec_a[s, h] = kd[:c].astype(BF16)
        q2[s, h] = jnp.where(in_a, qi, qi * dc_a).astype(BF16)
        k2[s, h] = jnp.where(in_a, kd * dc_b, kd).astype(BF16)
        dc_pair[s, h] = dc_a * dc_b
    sc = {u: lax.dot_general(q_in[u], jnp.concatenate([k_in[u], k_dec_a[u]], axis=0), nt,
                             preferred_element_type=F32) for u in units}
    zero_cc = jnp.zeros((c, c), F32)
    p = {u: jnp.concatenate(
            [jnp.concatenate([jnp.where(causal, sc[u][:c, :c], 0.0), zero_cc], axis=1),
             jnp.concatenate([sc[u][c:, 2 * c:], jnp.where(causal, sc[u][c:, c:2 * c], 0.0)],
                             axis=1)], axis=0).astype(BF16) for u in units}
    intra = {(s, h): _dot(p[s, h], v_all[rows_of[s], h * hv:(h + 1) * hv]) for s, h in units}
    row_parts = []
    for s in pairs:
        rows = rows_of[s]
        o = [intra[s, h] + lax.dot_general(q2[s, h], st_ref[h].astype(BF16), nt,
                                           preferred_element_type=F32) for h in heads]
        for h in heads:
            vc = slice(h * hv, (h + 1) * hv)
            st_ref[h] = dc_pair[s, h] * st_ref[h] + lax.dot_general(
                v_all[rows, vc], k2[s, h], tn, preferred_element_type=F32)
        parts = []
        for h in heads:
            vc = slice(h * hv, (h + 1) * hv)
            on = o[h] * lax.rsqrt(jnp.mean(o[h] * o[h], axis=-1, keepdims=True) + EPS)
            parts.append((on * gate_all[rows, vc]).astype(BF16))
        row_parts.append(jnp.concatenate(parts, axis=-1))
    a = jnp.concatenate(row_parts, axis=0)
    o_ref[...] = x + _dot(a, wout_ref[...])


def _gla_layer(x, seq, norm_w, w_in, w_gk_up, b_gk, out_norm_w, w_out, *, tm=256):
    t, d = x.shape
    rank, dk = w_gk_up.shape
    dv = out_norm_w.shape[0]
    tm = _tile(seq, tm)
    hk, hv = dk // GLA_HEADS, dv // GLA_HEADS
    n_main = 2 * dk + 2 * dv
    assert n_main % LANES == 0
    rank_pad = -(-rank // LANES) * LANES
    w_main = w_in.astype(BF16)
    w_low = jnp.pad(w_in[:, n_main:], ((0, 0), (0, rank_pad - rank))).astype(BF16)
    w_up = jnp.pad(w_gk_up, ((0, rank_pad - rank), (0, 0))).astype(BF16)
    return pl.pallas_call(
        functools.partial(_gla_layer_kernel, tiles_per_seq=seq // tm),
        grid=(t // tm,),
        in_specs=[pl.BlockSpec((tm, d), lambda i: (i, 0)),
                  _resident_spec((1, d)),
                  _resident_spec((d, n_main)),
                  _resident_spec((d, rank_pad)),
                  _resident_spec((rank_pad, dk)),
                  _resident_spec((1, dk)),
                  _resident_spec((1, dv)),
                  _resident_spec((dv, d))],
        out_specs=pl.BlockSpec((tm, d), lambda i: (i, 0)),
        out_shape=jax.ShapeDtypeStruct((t, d), F32),
        scratch_shapes=[pltpu.VMEM((GLA_HEADS, hv, hk), F32)],
        compiler_params=_params("arbitrary"),
        name="gla_layer",
    )(x, norm_w.reshape(1, d), w_main, w_low, w_up, b_gk.reshape(1, dk),
      out_norm_w.reshape(1, dv), w_out.astype(BF16))


def kernel(x, l0_norm1, l0_pool_w_in, l0_pool_w_group, l0_pool_scale, l0_pool_w_out, l0_norm2, l0_ffn_w_gate, l0_ffn_w_up, l0_ffn_w_down, l1_norm1, l1_sgu_w_in, l1_sgu_v_norm_w, l1_sgu_v_norm_b, l1_sgu_w_spatial, l1_sgu_b_spatial, l1_sgu_w_out, l1_norm2, l1_ffn_w_gate, l1_ffn_w_up, l1_ffn_w_down, l2_norm1, l2_gla_w_in, l2_gla_w_gk_up, l2_gla_b_gk, l2_gla_out_norm_w, l2_gla_w_out, l2_norm2, l2_ffn_w_gate, l2_ffn_w_up, l2_ffn_w_down, l3_norm1, l3_pool_w_in, l3_pool_w_group, l3_pool_scale, l3_pool_w_out, l3_norm2, l3_ffn_w_gate, l3_ffn_w_up, l3_ffn_w_down, final_norm_w):
    batch, seq, d = x.shape
    h = x.reshape(batch * seq, d)
    h, ffn0 = _pool_layer(h, seq, l0_norm1, l0_pool_w_in, l0_pool_w_group, l0_pool_scale,
                          l0_pool_w_out, (l0_ffn_w_gate, l0_ffn_w_up, l0_ffn_w_down))
    h, (sgu_w_in, sgu_w_out) = _ffn(h, l0_norm2, *ffn0,
                                    cast_weights=(l1_sgu_w_in, l1_sgu_w_out))
    h, ffn1 = _sgu_layer(h, l1_norm1, sgu_w_in, l1_sgu_v_norm_w, l1_sgu_v_norm_b,
                         l1_sgu_w_spatial, l1_sgu_b_spatial, sgu_w_out,
                         (l1_ffn_w_gate, l1_ffn_w_up, l1_ffn_w_down))
    h, (gla_w_out,) = _ffn(h, l1_norm2, *ffn1, cast_weights=(l2_gla_w_out,))
    h = _gla_layer(h, seq, l2_norm1, l2_gla_w_in, l2_gla_w_gk_up, l2_gla_b_gk,
                   l2_gla_out_norm_w, gla_w_out)
    group_shape = l3_pool_w_group.shape
    h, (pool_w_in, pool_w_group, pool_w_out) = _ffn(
        h, l2_norm2, l2_ffn_w_gate, l2_ffn_w_up, l2_ffn_w_down,
        cast_weights=(l3_pool_w_in, l3_pool_w_group.reshape(-1, group_shape[-1]),
                      l3_pool_w_out))
    h, ffn3 = _pool_layer(h, seq, l3_norm1, pool_w_in, pool_w_group.reshape(group_shape),
                          l3_pool_scale, pool_w_out,
                          (l3_ffn_w_gate, l3_ffn_w_up, l3_ffn_w_down))
    h, _ = _ffn(h, l3_norm2, *ffn3, final_norm_w)
    return h.reshape(batch, seq, d)
```

```python
import functools

import jax
import jax.numpy as jnp
from jax import lax
from jax.experimental import pallas as pl
from jax.experimental.pallas import tpu as pltpu

F32 = jnp.float32
BF16 = jnp.bfloat16

EPS = 1e-6
POOL_WINDOWS = (2, 4, 8, 16)
POOL_HALO = 24
SGU_CHUNK = 128
GLA_HEADS = 4
GLA_CHUNK = 64
GLA_GATE_TEMP = 16.0
LANES = 128
VMEM_LIMIT = 58 * 1024 * 1024


def _params(*semantics):
    return pltpu.CompilerParams(dimension_semantics=semantics,
                                vmem_limit_bytes=VMEM_LIMIT)


def _tile(n, want):
    t = min(n, want)
    assert n % t == 0, (n, want)
    return t


def _rms_norm(x, w):
    ms = jnp.mean(x * x, axis=-1, keepdims=True)
    return x * lax.rsqrt(ms + EPS) * w


def _dot(a, b):
    return jnp.dot(a, b, preferred_element_type=F32)


def _const_spec(shape):
    return pl.BlockSpec(shape, lambda *_: (0,) * len(shape))


def _resident_spec(shape):
    return pl.BlockSpec(shape, lambda *_: (0,) * len(shape),
                        pipeline_mode=pl.Buffered(1))


BF16_SUBLANES = 16


def _cast_plan(steps, weights):
    in_specs, out_specs, out_shapes = [], [], []
    for w in weights:
        rows, cols = w.shape
        rep = 1
        while (rows * rep) % (steps * BF16_SUBLANES):
            rep *= 2
            assert rep <= steps
        blk = rows * rep // steps

        def index_map(i, *_, rep=rep):
            return (i // rep, 0)

        in_specs.append(pl.BlockSpec((blk, cols), index_map))
        out_specs.append(pl.BlockSpec((blk, cols), index_map))
        out_shapes.append(jax.ShapeDtypeStruct((rows, cols), BF16))
    return in_specs, out_specs, out_shapes


def _cast_blocks(src_refs, dst_refs):
    for src, dst in zip(src_refs, dst_refs):
        dst[...] = src[...].astype(BF16)


def _pool_layer_kernel(x_ref, nw_ref, win_ref, wg_ref, sc_ref, wout_ref, c0, c1, c2,
                       o_ref, d0, d1, d2, zbuf, *, tiles_per_seq, sub):
    _cast_blocks((c0, c1, c2), (d0, d1, d2))
    tm = x_ref.shape[0]
    cg = wg_ref.shape[1]
    pos = pl.program_id(0) % tiles_per_seq
    halo = POOL_HALO

    @pl.when(pos == 0)
    def _():
        zbuf[0:halo, :] = jnp.zeros((halo, zbuf.shape[1]), F32)

    @pl.when(pos != 0)
    def _():
        zbuf[0:halo, :] = zbuf[tm:tm + halo, :]

    ext = sub + 16
    for r0 in range(0, tm, sub):
        h = _rms_norm(x_ref[r0:r0 + sub, :], nw_ref[...]).astype(BF16)
        zbuf[halo + r0:halo + r0 + sub, :] = _dot(h, win_ref[...])
    for r0 in range(0, tm, sub):
        x = x_ref[r0:r0 + sub, :]
        t_idx = pos * tm + r0 + lax.broadcasted_iota(jnp.int32, (sub, cg), 0)
        ys = []
        for g, w in enumerate(POOL_WINDOWS):
            cols = slice(g * cg, (g + 1) * cg)
            z = zbuf[halo + r0:halo + r0 + sub, cols]
            s = zbuf[8 + r0:8 + r0 + ext, cols]
            shift = 1
            while shift < w:
                s = s + pltpu.roll(s, shift, 0)
                shift *= 2
            count = jnp.minimum(t_idx + 1, w).astype(F32)
            p = s[16:, :] / count - z
            y = _dot(p.astype(BF16), wg_ref[g]) * sc_ref[:, cols]
            ys.append(y.astype(BF16))
        y = jnp.concatenate(ys, axis=-1)
        o_ref[r0:r0 + sub, :] = x + _dot(y, wout_ref[...])


def _pool_layer(x, seq, norm_w, w_in, w_group, scale, w_out, ffn_weights, *, tm=512, sub=256):
    t, d = x.shape
    p = w_in.shape[1]
    g, cg, _ = w_group.shape
    tm = _tile(seq, tm)
    sub = _tile(tm, sub)
    cast_in, cast_out, cast_shapes = _cast_plan(t // tm, ffn_weights)
    out, *ffn_bf16 = pl.pallas_call(
        functools.partial(_pool_layer_kernel, tiles_per_seq=seq // tm, sub=sub),
        grid=(t // tm,),
        in_specs=[pl.BlockSpec((tm, d), lambda i: (i, 0)),
                  _resident_spec((1, d)),
                  _resident_spec((d, p)),
                  _resident_spec((g, cg, cg)),
                  _resident_spec((1, p)),
                  _resident_spec((p, d))] + cast_in,
        out_specs=[pl.BlockSpec((tm, d), lambda i: (i, 0))] + cast_out,
        out_shape=[jax.ShapeDtypeStruct((t, d), F32)] + cast_shapes,
        scratch_shapes=[pltpu.VMEM((tm + POOL_HALO, p), F32)],
        compiler_params=_params("arbitrary"),
        name="pool_layer",
    )(x, norm_w.reshape(1, d), w_in.astype(BF16), w_group.astype(BF16),
      scale.reshape(1, p), w_out.astype(BF16), *ffn_weights)
    return out, ffn_bf16


FFN_NORM_STEPS = 4


def _ffn_kernel(x_hbm, nw_ref, wg_ref, wu_ref, wd_ref, *rest, n_cast, final_norm):
    rest = list(rest)
    fnw_ref = rest.pop(0) if final_norm else None
    cast_src = [rest.pop(0) for _ in range(n_cast)]
    o_hbm = rest.pop(0)
    cast_dst = [rest.pop(0) for _ in range(n_cast)]
    acc0, acc1, hbuf0, hbuf1, in_sem, out_sem = rest
    acc, hbuf = (acc0, acc1), (hbuf0, hbuf1)
    i, j = pl.program_id(0), pl.program_id(1)
    nt, nj = pl.num_programs(0), pl.num_programs(1)
    tm = acc0.shape[0]
    rq = tm // FFN_NORM_STEPS
    q = j - (nj - FFN_NORM_STEPS)
    has_next = i + 1 < nt

    def x_copy(tile, s):
        return pltpu.make_async_copy(x_hbm.at[pl.ds(tile * tm, tm), :], acc[s], in_sem.at[s])

    def o_copy(tile, s):
        return pltpu.make_async_copy(acc[s], o_hbm.at[pl.ds(tile * tm, tm), :], out_sem.at[s])

    if n_cast:
        @pl.when(j == 0)
        def _():
            _cast_blocks(cast_src, cast_dst)

    @pl.when((i == 0) & (j == 0))
    def _():
        first = x_copy(0, 0)
        first.start()
        first.wait()
        hbuf[0][...] = _rms_norm(acc[0][...], nw_ref[...]).astype(BF16)

    def step(slot, with_norm):
        other = 1 - slot
        h = hbuf[slot][...]
        tfh = wg_ref.shape[1] // 2
        rp = rq // 4

        def norm_piece(p):
            if with_norm:
                rows = pl.ds(pl.multiple_of(q * rq + p * rp, rp), rp)
                hbuf[other][rows, :] = _rms_norm(acc[other][rows, :], nw_ref[...]).astype(BF16)

        parts = []
        for half in range(2):
            cols = slice(half * tfh, (half + 1) * tfh)
            g = _dot(h, wg_ref[:, cols].astype(BF16))
            norm_piece(2 * half)
            u = _dot(h, wu_ref[:, cols].astype(BF16))
            norm_piece(2 * half + 1)
            parts.append((g * jax.nn.sigmoid(g) * u).astype(BF16))
        a = jnp.concatenate(parts, axis=-1)
        acc[slot][...] += _dot(a, wd_ref[...].astype(BF16))

    for slot in (0, 1):
        other = 1 - slot
        mine = (i % 2) == slot

        @pl.when(mine & (j == 1) & (i >= 1))
        def _():
            o_copy(i - 1, other).wait()

        @pl.when(mine & (j == 1) & has_next)
        def _():
            x_copy(i + 1, other).start()

        @pl.when(mine & (q == 0) & has_next)
        def _():
            x_copy(i + 1, other).wait()

        norm_next = (q >= 0) & has_next
        pl.when(mine & norm_next)(functools.partial(step, slot, True))
        pl.when(mine & jnp.logical_not(norm_next))(functools.partial(step, slot, False))

        @pl.when(mine & (j == nj - 1))
        def _():
            if fnw_ref is not None:
                acc[slot][...] = _rms_norm(acc[slot][...], fnw_ref[...])
            o_copy(i, slot).start()

        @pl.when(mine & (j == nj - 1) & (i == nt - 1))
        def _():
            o_copy(i, slot).wait()


def _ffn(x, norm_w, w_gate, w_up, w_down, final_norm_w=None, cast_weights=(),
         *, tm=1024, tf=512):
    t, d = x.shape
    f = w_gate.shape[1]
    tm, tf = _tile(t, tm), _tile(f, tf)
    nt, nj = t // tm, f // tf
    assert nt >= 2 and nj >= FFN_NORM_STEPS + 2 and tm % FFN_NORM_STEPS == 0
    in_specs = [pl.BlockSpec(memory_space=pl.ANY),
                _const_spec((1, d)),
                pl.BlockSpec((d, tf), lambda i, j: (0, j)),
                pl.BlockSpec((d, tf), lambda i, j: (0, j)),
                pl.BlockSpec((tf, d), lambda i, j: (j, 0))]
    args = [x, norm_w.reshape(1, d), w_gate, w_up, w_down]
    if final_norm_w is not None:
        in_specs.append(_const_spec((1, d)))
        args.append(final_norm_w.reshape(1, d))
    cast_in, cast_out, cast_shapes = _cast_plan(nt, cast_weights)
    out, *cast_bf16 = pl.pallas_call(
        functools.partial(_ffn_kernel, n_cast=len(cast_weights),
                          final_norm=final_norm_w is not None),
        grid=(nt, nj),
        in_specs=in_specs + cast_in,
        out_specs=[pl.BlockSpec(memory_space=pl.ANY)] + cast_out,
        out_shape=[jax.ShapeDtypeStruct((t, d), F32)] + cast_shapes,
        scratch_shapes=[pltpu.VMEM((tm, d), F32), pltpu.VMEM((tm, d), F32),
                        pltpu.VMEM((tm, d), BF16), pltpu.VMEM((tm, d), BF16),
                        pltpu.SemaphoreType.DMA((2,)),
                        pltpu.SemaphoreType.DMA((2,))],
        compiler_params=_params("arbitrary", "arbitrary"),
        name="ffn",
    )(*args, *cast_weights)
    return out, cast_bf16


def _gelu(z):
    return 0.5 * z * (1.0 + lax.erf(z * (2.0 ** -0.5)))


def _sgu_layer_kernel(x_ref, nw_ref, win_ref, lnw_ref, lnb_ref, ws_ref, bs_ref, wout_ref,
                      c0, c1, c2, o_ref, d0, d1, d2, *, sub):
    _cast_blocks((c0, c1, c2), (d0, d1, d2))
    tm = x_ref.shape[0]
    dim = lnw_ref.shape[1]
    heads = ws_ref.shape[0]
    hd = dim // heads
    c = SGU_CHUNK

    row = lax.broadcasted_iota(jnp.int32, (c, c), 0)
    col = lax.broadcasted_iota(jnp.int32, (c, c), 1)
    causal = col <= row
    wms = [jnp.where(causal, ws_ref[h], 0.0).astype(BF16) for h in range(heads)]

    for r0 in range(0, tm, sub):
        x = x_ref[r0:r0 + sub, :]
        h = _rms_norm(x, nw_ref[...]).astype(BF16)
        v = _gelu(_dot(h, win_ref[:, dim:]))
        mu = jnp.mean(v, axis=-1, keepdims=True)
        vc = v - mu
        var = jnp.mean(vc * vc, axis=-1, keepdims=True)
        vn = (vc * lax.rsqrt(var + EPS) * lnw_ref[...] + lnb_ref[...]).astype(BF16)
        parts = []
        u_all = _gelu(_dot(h, win_ref[:, :dim]))
        for hh in range(heads):
            cols = slice(hh * hd, (hh + 1) * hd)
            bias = bs_ref[:, hh:hh + 1]
            gate = jnp.concatenate(
                [_dot(wms[hh], vn[n * c:(n + 1) * c, cols]) + bias for n in range(sub // c)],
                axis=0)
            parts.append((u_all[:, cols] * gate).astype(BF16))
        a = jnp.concatenate(parts, axis=-1)
        o_ref[r0:r0 + sub, :] = x + _dot(a, wout_ref[...])


def _sgu_layer(x, norm_w, w_in, ln_w, ln_b, w_spatial, b_spatial, w_out, ffn_weights,
               *, tm=512, sub=256):
    t, d = x.shape
    dim = w_out.shape[0]
    heads, c, _ = w_spatial.shape
    tm = _tile(t, tm)
    sub = _tile(tm, sub)
    cast_in, cast_out, cast_shapes = _cast_plan(t // tm, ffn_weights)
    out, *ffn_bf16 = pl.pallas_call(
        functools.partial(_sgu_layer_kernel, sub=sub),
        grid=(t // tm,),
        in_specs=[pl.BlockSpec((tm, d), lambda i: (i, 0)),
                  _resident_spec((1, d)),
                  _resident_spec((d, 2 * dim)),
                  _resident_spec((1, dim)),
                  _resident_spec((1, dim)),
                  _resident_spec((heads, c, c)),
                  _resident_spec((c, heads)),
                  _resident_spec((dim, d))] + cast_in,
        out_specs=[pl.BlockSpec((tm, d), lambda i: (i, 0))] + cast_out,
        out_shape=[jax.ShapeDtypeStruct((t, d), F32)] + cast_shapes,
        compiler_params=_params("arbitrary"),
        name="sgu_layer",
    )(x, norm_w.reshape(1, d), w_in.astype(BF16), ln_w.reshape(1, dim), ln_b.reshape(1, dim),
      w_spatial, b_spatial.T, w_out.astype(BF16), *ffn_weights)
    return out, ffn_bf16


def _gla_layer_kernel(x_ref, nw_ref, w_ref, wlo_ref, wup_ref, bgk_ref, onw_ref, wout_ref,
                      o_ref, st_ref, *, tiles_per_seq):
    tm = x_ref.shape[0]
    dkt = wup_ref.shape[1]
    dvt = onw_ref.shape[1]
    hk, hv = dkt // GLA_HEADS, dvt // GLA_HEADS
    c = GLA_CHUNK
    heads = range(GLA_HEADS)

    @pl.when(pl.program_id(0) % tiles_per_seq == 0)
    def _():
        st_ref[...] = jnp.zeros(st_ref.shape, F32)

    row_k = lax.broadcasted_iota(jnp.int32, (2 * c, hk), 0)
    in_a = row_k < c
    row_in_chunk = jnp.where(in_a, row_k, row_k - c)
    row = lax.broadcasted_iota(jnp.int32, (c, c), 0)
    col = lax.broadcasted_iota(jnp.int32, (c, c), 1)
    causal = col <= row
    nt = (((1,), (1,)), ((), ()))
    tn = (((0,), (0,)), ((), ()))

    x = x_ref[...]
    hn = _rms_norm(x, nw_ref[...]).astype(BF16)
    low = _dot(hn, wlo_ref[...]).astype(BF16)
    q_all = _dot(hn, w_ref[:, 0:dkt]) * (hk ** -0.5)
    k_all = _dot(hn, w_ref[:, dkt:2 * dkt])
    g_all = jax.nn.log_sigmoid(_dot(low, wup_ref[...]) + bgk_ref[...]) / GLA_GATE_TEMP
    v_all = _dot(hn, w_ref[:, 2 * dkt:2 * dkt + dvt]).astype(BF16)
    r_all = _dot(hn, w_ref[:, 2 * dkt + dvt:])
    gate_all = r_all * jax.nn.sigmoid(r_all) * onw_ref[...]

    pairs = range(tm // (2 * c))
    rows_of = [slice(s * 2 * c, (s + 1) * 2 * c) for s in pairs]
    units = [(s, h) for s in pairs for h in heads]
    q_in, k_in, k_dec_a, q2, k2, dc_pair = {}, {}, {}, {}, {}, {}
    for s, h in units:
        rows = rows_of[s]
        kc = slice(h * hk, (h + 1) * hk)
        b = g_all[rows, kc]
        shift = 1
        while shift < c:
            b = b + jnp.where(row_in_chunk >= shift, pltpu.roll(b, shift, 0), 0.0)
            shift *= 2
        b_last_a, b_last_b = b[c - 1:c, :], b[2 * c - 1:2 * c, :]
        dc_a, dc_b = jnp.exp(b_last_a), jnp.exp(b_last_b)
        k = k_all[rows, kc]
        qi = q_all[rows, kc] * jnp.exp(b)
        kd = k * jnp.exp(jnp.where(in_a, b_last_a, b_last_b) - b)
        q_in[s, h] = qi.astype(BF16)
        k_in[s, h] = (k * jnp.exp(-b)).astype(BF16)
        k_dec_a[s, h] = kd[:c].astype(BF16)
        q2[s, h] = jnp.where(in_a, qi, qi * dc_a)
        k2[s, h] = jnp.where(in_a, kd * dc_b, kd)
        dc_pair[s, h] = dc_a * dc_b
    sc = {u: lax.dot_general(q_in[u], jnp.concatenate([k_in[u], k_dec_a[u]], axis=0), nt,
                             preferred_element_type=F32) for u in units}
    zero_cc = jnp.zeros((c, c), F32)
    p = {u: jnp.concatenate(
            [jnp.concatenate([jnp.where(causal, sc[u][:c, :c], 0.0), zero_cc], axis=1),
             jnp.concatenate([sc[u][c:, 2 * c:], jnp.where(causal, sc[u][c:, c:2 * c], 0.0)],
                             axis=1)], axis=0).astype(BF16) for u in units}
    assert len(pairs) == 2
    cross = {h: lax.dot_general(q2[1, h].astype(BF16), k2[0, h].astype(BF16), nt,
                                preferred_element_type=F32) for h in heads}
    zero_pp = jnp.zeros((2 * c, 2 * c), BF16)
    p_tile = {h: jnp.concatenate(
        [jnp.concatenate([p[0, h], zero_pp], axis=1),
         jnp.concatenate([cross[h].astype(BF16), p[1, h]], axis=1)], axis=0) for h in heads}
    q_tile = {h: jnp.concatenate([q2[0, h], q2[1, h] * dc_pair[0, h]], axis=0).astype(BF16)
              for h in heads}
    k_tile = {h: jnp.concatenate([k2[0, h] * dc_pair[1, h], k2[1, h]], axis=0).astype(BF16)
              for h in heads}
    o = {h: _dot(p_tile[h], v_all[:, h * hv:(h + 1) * hv])
         + lax.dot_general(q_tile[h], st_ref[h].astype(BF16), nt, preferred_element_type=F32)
         for h in heads}
    for h in heads:
        st_ref[h] = (dc_pair[0, h] * dc_pair[1, h]) * st_ref[h] + lax.dot_general(
            v_all[:, h * hv:(h + 1) * hv], k_tile[h], tn, preferred_element_type=F32)
    parts = []
    for h in heads:
        vc = slice(h * hv, (h + 1) * hv)
        on = o[h] * lax.rsqrt(jnp.mean(o[h] * o[h], axis=-1, keepdims=True) + EPS)
        parts.append((on * gate_all[:, vc]).astype(BF16))
    a = jnp.concatenate(parts, axis=-1)
    o_ref[...] = x + _dot(a, wout_ref[...])


def _gla_layer(x, seq, norm_w, w_in, w_gk_up, b_gk, out_norm_w, w_out, *, tm=256):
    t, d = x.shape
    rank, dk = w_gk_up.shape
    dv = out_norm_w.shape[0]
    tm = _tile(seq, tm)
    hk, hv = dk // GLA_HEADS, dv // GLA_HEADS
    n_main = 2 * dk + 2 * dv
    assert n_main % LANES == 0
    rank_pad = -(-rank // LANES) * LANES
    w_main = w_in.astype(BF16)
    w_low = jnp.pad(w_in[:, n_main:], ((0, 0), (0, rank_pad - rank))).astype(BF16)
    w_up = jnp.pad(w_gk_up, ((0, rank_pad - rank), (0, 0))).astype(BF16)
    return pl.pallas_call(
        functools.partial(_gla_layer_kernel, tiles_per_seq=seq // tm),
        grid=(t // tm,),
        in_specs=[pl.BlockSpec((tm, d), lambda i: (i, 0)),
                  _resident_spec((1, d)),
                  _resident_spec((d, n_main)),
                  _resident_spec((d, rank_pad)),
                  _resident_spec((rank_pad, dk)),
                  _resident_spec((1, dk)),
                  _resident_spec((1, dv)),
                  _resident_spec((dv, d))],
        out_specs=pl.BlockSpec((tm, d), lambda i: (i, 0)),
        out_shape=jax.ShapeDtypeStruct((t, d), F32),
        scratch_shapes=[pltpu.VMEM((GLA_HEADS, hv, hk), F32)],
        compiler_params=_params("arbitrary"),
        name="gla_layer",
    )(x, norm_w.reshape(1, d), w_main, w_low, w_up, b_gk.reshape(1, dk),
      out_norm_w.reshape(1, dv), w_out.astype(BF16))


def kernel(x, l0_norm1, l0_pool_w_in, l0_pool_w_group, l0_pool_scale, l0_pool_w_out, l0_norm2, l0_ffn_w_gate, l0_ffn_w_up, l0_ffn_w_down, l1_norm1, l1_sgu_w_in, l1_sgu_v_norm_w, l1_sgu_v_norm_b, l1_sgu_w_spatial, l1_sgu_b_spatial, l1_sgu_w_out, l1_norm2, l1_ffn_w_gate, l1_ffn_w_up, l1_ffn_w_down, l2_norm1, l2_gla_w_in, l2_gla_w_gk_up, l2_gla_b_gk, l2_gla_out_norm_w, l2_gla_w_out, l2_norm2, l2_ffn_w_gate, l2_ffn_w_up, l2_ffn_w_down, l3_norm1, l3_pool_w_in, l3_pool_w_group, l3_pool_scale, l3_pool_w_out, l3_norm2, l3_ffn_w_gate, l3_ffn_w_up, l3_ffn_w_down, final_norm_w):
    batch, seq, d = x.shape
    h = x.reshape(batch * seq, d)
    h, ffn0 = _pool_layer(h, seq, l0_norm1, l0_pool_w_in, l0_pool_w_group, l0_pool_scale,
                          l0_pool_w_out, (l0_ffn_w_gate, l0_ffn_w_up, l0_ffn_w_down))
    h, (sgu_w_in, sgu_w_out) = _ffn(h, l0_norm2, *ffn0,
                                    cast_weights=(l1_sgu_w_in, l1_sgu_w_out))
    h, ffn1 = _sgu_layer(h, l1_norm1, sgu_w_in, l1_sgu_v_norm_w, l1_sgu_v_norm_b,
                         l1_sgu_w_spatial, l1_sgu_b_spatial, sgu_w_out,
                         (l1_ffn_w_gate, l1_ffn_w_up, l1_ffn_w_down))
    h, (gla_w_out,) = _ffn(h, l1_norm2, *ffn1, cast_weights=(l2_gla_w_out,))
    h = _gla_layer(h, seq, l2_norm1, l2_gla_w_in, l2_gla_w_gk_up, l2_gla_b_gk,
                   l2_gla_out_norm_w, gla_w_out)
    group_shape = l3_pool_w_group.shape
    h, (pool_w_in, pool_w_group, pool_w_out) = _ffn(
        h, l2_norm2, l2_ffn_w_gate, l2_ffn_w_up, l2_ffn_w_down,
        cast_weights=(l3_pool_w_in, l3_pool_w_group.reshape(-1, group_shape[-1]),
                      l3_pool_w_out))
    h, ffn3 = _pool_layer(h, seq, l3_norm1, pool_w_in, pool_w_group.reshape(group_shape),
                          l3_pool_scale, pool_w_out,
                          (l3_ffn_w_gate, l3_ffn_w_up, l3_ffn_w_down))
    h, _ = _ffn(h, l3_norm2, *ffn3, final_norm_w)
    return h.reshape(batch, seq, d)
```
